```python
import numpy as np
import jax
import jax.numpy as jnp
from jax import lax

D_MODEL = 1024
BATCH = 4
SEQ = 8192
DEPTH = 4

f32 = jnp.float32

CHUNK = 64
Q_BLOCK = 128
ROPE_THETA = 10000.0
LN_EPS = 1e-5
RMS_EPS = 1e-6
NEG_INF = -1e30

ATTN_HEAD_DIM = 64

GLA_HEADS = 4
GLA_DK = 64
GLA_DV = 128
GLA_GATE_RANK = 16
GLA_TAU = 16.0

DIFF_HEADS = 4
DIFF_DH = ATTN_HEAD_DIM
DIFF_DV = 2 * DIFF_DH

SWA_Q_HEADS = 16
SWA_KV_HEADS = 2
SWA_DH = ATTN_HEAD_DIM
SWA_GROUP = SWA_Q_HEADS // SWA_KV_HEADS
WINDOW = 128
WINDOW_CHUNKS = WINDOW // CHUNK

MEM_LEN = 256
MEM_HEADS = 4
MEM_DH = D_MODEL // MEM_HEADS

D_FF = 2816
CONV_WIDTH = 3

DEEPNORM_ALPHA = (2 * DEPTH) ** 0.25
DEEPNORM_BETA = (8 * DEPTH) ** -0.25

N_EVEN = (DEPTH + 1) // 2
N_ODD = DEPTH // 2

EVEN_SPLITS = [GLA_HEADS * GLA_DK, GLA_HEADS * GLA_DK, GLA_HEADS * GLA_DV, GLA_HEADS * GLA_DV,
               GLA_GATE_RANK, DIFF_HEADS * 2 * DIFF_DH, DIFF_HEADS * 2 * DIFF_DH, DIFF_HEADS * DIFF_DV]
EVEN_IN = sum(EVEN_SPLITS)
EVEN_MIX = GLA_HEADS * GLA_DV + DIFF_HEADS * DIFF_DV
ODD_SPLITS = [SWA_Q_HEADS * SWA_DH, SWA_KV_HEADS * SWA_DH, SWA_KV_HEADS * SWA_DH]
ODD_IN = sum(ODD_SPLITS)
ODD_MIX = SWA_Q_HEADS * SWA_DH

kernel_name = "hybrid_gla_diff_swa_streaming_encoder"


def _split(t, sizes):
    return jnp.split(t, np.cumsum(sizes)[:-1].tolist(), axis=-1)


def layer_norm(x, g, b):
    xf = x.astype(f32)
    mu = jnp.mean(xf, -1, keepdims=True)
    var = jnp.mean(jnp.square(xf - mu), -1, keepdims=True)
    return ((xf - mu) * lax.rsqrt(var + LN_EPS) * g.astype(f32) + b.astype(f32)).astype(x.dtype)


def rms_norm(x, g):
    xf = x.astype(f32)
    return xf * lax.rsqrt(jnp.mean(xf * xf, -1, keepdims=True) + RMS_EPS) * g.astype(f32)


def rope_tables(positions, dim):
    inv_freq = ROPE_THETA ** (-jnp.arange(0, dim, 2, dtype=f32) / dim)
    ang = positions.astype(f32)[..., None] * inv_freq
    return jnp.cos(ang), jnp.sin(ang)


def apply_rope(t, cos, sin):
    t = t.astype(f32)
    t1, t2 = jnp.split(t, 2, axis=-1)
    c = cos[:, :, None, :]
    s = sin[:, :, None, :]
    return jnp.concatenate([t1 * c - t2 * s, t2 * c + t1 * s], axis=-1)


def gla_mixer(q, k, v, r, gate_code, gate_w, gate_b, norm_g):
    B, S, H, DK = q.shape
    DV = v.shape[-1]
    N = S // CHUNK
    log_a = jax.nn.log_sigmoid((gate_code @ gate_w + gate_b).astype(f32)) / GLA_TAU
    b = jnp.cumsum(log_a.reshape(B, N, CHUNK, H, DK), axis=2)
    qf = q.astype(f32).reshape(B, N, CHUNK, H, DK) * (DK ** -0.5)
    kf = k.astype(f32).reshape(B, N, CHUNK, H, DK)
    vf = v.astype(f32).reshape(B, N, CHUNK, H, DV)
    q_dec = qf * jnp.exp(b)
    causal = jnp.tril(jnp.ones((CHUNK, CHUNK), dtype=bool))
    a_intra = jnp.where(causal, jnp.einsum('bnihd,bnjhd->bnhij', q_dec, kf * jnp.exp(-b)), 0.0)
    o = jnp.einsum('bnhij,bnjhe->bnihe', a_intra, vf)
    b_last = b[:, :, -1]
    kv = jnp.einsum('bnchd,bnche->nbhde', kf * jnp.exp(b_last[:, :, None] - b), vf)
    decay = jnp.moveaxis(jnp.exp(b_last), 1, 0)

    def step(state, inp):
        kv_n, dec_n = inp
        return dec_n[..., None] * state + kv_n, state

    _, states = lax.scan(step, jnp.zeros((B, H, DK, DV), f32), (kv, decay))
    o = o + jnp.einsum('bnihd,nbhde->bnihe', q_dec, states)
    o = o.reshape(B, S, H, DV)
    return rms_norm(o, norm_g) * jax.nn.silu(r.astype(f32))


def diff_attention(q, k, v, lq1, lk1, lq2, lk2, norm_g, lam_init, cos, sin):
    B, S = q.shape[0], q.shape[1]
    q = apply_rope(q, cos, sin).reshape(B, S, DIFF_HEADS, 2, DIFF_DH)
    k = apply_rope(k, cos, sin).reshape(B, S, DIFF_HEADS, 2, DIFF_DH)
    vf = v.astype(f32)
    lam = (jnp.exp(jnp.sum(lq1.astype(f32) * lk1.astype(f32)))
           - jnp.exp(jnp.sum(lq2.astype(f32) * lk2.astype(f32))) + lam_init)
    n_blocks = S // Q_BLOCK
    q_blocks = jnp.moveaxis(q.reshape(B, n_blocks, Q_BLOCK, DIFF_HEADS, 2, DIFF_DH), 1, 0)
    key_chunk = jnp.arange(S) // CHUNK
    scale = DIFF_DH ** -0.5

    def one_block(args):
        q_blk, blk = args
        s = jnp.einsum('bqhtd,bkhtd->bhtqk', q_blk, k) * scale
        q_chunk = (blk * Q_BLOCK + jnp.arange(Q_BLOCK)) // CHUNK
        visible = key_chunk[None, :] <= q_chunk[:, None]
        p = jax.nn.softmax(jnp.where(visible, s, NEG_INF), axis=-1)
        a = p[:, :, 0] - lam * p[:, :, 1]
        return jnp.einsum('bhqk,bkhe->bqhe', a, vf)

    o = lax.map(one_block, (q_blocks, jnp.arange(n_blocks)))
    o = jnp.moveaxis(o, 0, 1).reshape(B, S, DIFF_HEADS, DIFF_DV)
    return rms_norm(o, norm_g) * (1.0 - lam_init)


def banded_chunks(t):
    B, S, H, D = t.shape
    N = S // CHUNK
    tp = jnp.pad(t, ((0, 0), (WINDOW_CHUNKS * CHUNK, 0), (0, 0), (0, 0)))
    tp = tp.reshape(B, N + WINDOW_CHUNKS, CHUNK, H, D)
    return jnp.concatenate([tp[:, j:j + N] for j in range(WINDOW_CHUNKS + 1)], axis=2)


def swa_sink_attention(q, k, v, sinks):
    B, S = q.shape[0], q.shape[1]
    N = S // CHUNK
    qc = q.reshape(B, N, CHUNK, SWA_KV_HEADS, SWA_GROUP, SWA_DH)
    kb = banded_chunks(k)
    vb = banded_chunks(v)
    s = jnp.einsum('bnikgd,bnjkd->bnkgij', qc, kb) * (SWA_DH ** -0.5)
    key_chunk = (jnp.arange(N)[:, None] - WINDOW_CHUNKS
                 + (jnp.arange((WINDOW_CHUNKS + 1) * CHUNK) // CHUNK)[None, :])
    s = jnp.where((key_chunk >= 0)[None, :, None, None, None, :], s, NEG_INF)
    sink = sinks.astype(f32).reshape(SWA_KV_HEADS, SWA_GROUP)[None, None, :, :, None, None]
    m = jnp.maximum(jnp.max(s, -1, keepdims=True), sink)
    p = jnp.exp(s - m)
    p = p / (jnp.sum(p, -1, keepdims=True) + jnp.exp(sink - m))
    o = jnp.einsum('bnkgij,bnjkd->bnikgd', p, vb)
    return o.reshape(B, S, SWA_Q_HEADS, SWA_DH)


def even_mixer(x, cos, sin, w_in, w_out, gate_w, gate_b, gla_g, lq1, lk1, lq2, lk2, diff_g, lam_init):
    B, S, _ = x.shape
    qa, ka, va, ra, ga, qb, kb, vb = _split(x @ w_in, EVEN_SPLITS)
    o_a = gla_mixer(qa.reshape(B, S, GLA_HEADS, GLA_DK), ka.reshape(B, S, GLA_HEADS, GLA_DK),
                    va.reshape(B, S, GLA_HEADS, GLA_DV), ra.reshape(B, S, GLA_HEADS, GLA_DV),
                    ga, gate_w, gate_b, gla_g)
    o_b = diff_attention(qb.reshape(B, S, DIFF_HEADS * 2, DIFF_DH), kb.reshape(B, S, DIFF_HEADS * 2, DIFF_DH),
                         vb.reshape(B, S, DIFF_HEADS, DIFF_DV), lq1, lk1, lq2, lk2, diff_g, lam_init, cos, sin)
    o = jnp.concatenate([o_a.reshape(B, S, -1), o_b.reshape(B, S, -1)], axis=-1)
    return o.astype(x.dtype) @ w_out


def odd_mixer(x, cos, sin, w_in, w_out, sinks):
    B, S, _ = x.shape
    q, k, v = _split(x @ w_in, ODD_SPLITS)
    q = apply_rope(q.reshape(B, S, SWA_Q_HEADS, SWA_DH), cos, sin)
    k = apply_rope(k.reshape(B, S, SWA_KV_HEADS, SWA_DH), cos, sin)
    v = v.reshape(B, S, SWA_KV_HEADS, SWA_DH).astype(f32)
    o = swa_sink_attention(q, k, v, sinks)
    return o.reshape(B, S, ODD_MIX).astype(x.dtype) @ w_out


def memory_cross_attention(x, mem, w_q, w_kv, w_out):
    B, S, _ = x.shape
    M = mem.shape[1]
    q = (x @ w_q).reshape(B, S, MEM_HEADS, MEM_DH).astype(f32)
    k, v = jnp.split((mem @ w_kv).astype(f32), 2, axis=-1)
    k = k.reshape(B, M, MEM_HEADS, MEM_DH)
    v = v.reshape(B, M, MEM_HEADS, MEM_DH)
    p = jax.nn.softmax(jnp.einsum('bshd,bmhd->bhsm', q, k) * (MEM_DH ** -0.5), axis=-1)
    o = jnp.einsum('bhsm,bmhd->bshd', p, v).reshape(B, S, D_MODEL)
    return o.astype(x.dtype) @ w_out


def causal_depthwise_conv(t, w, b):
    C = t.shape[-1]
    y = lax.conv_general_dilated(t, w[:, None, :], window_strides=(1,), padding=[(CONV_WIDTH - 1, 0)],
                                 dimension_numbers=('NWC', 'WIO', 'NWC'), feature_group_count=C)
    return y + b


def conv_ffn(x, w_in, conv_w, conv_b, w_out):
    g, u = jnp.split(x @ w_in, 2, axis=-1)
    g = causal_depthwise_conv(g, conv_w, conv_b)
    h = jax.nn.gelu(g.astype(f32)) * u.astype(f32)
    return h.astype(x.dtype) @ w_out


def setup_inputs(seed: int = 0) -> dict:
    key = jax.random.key(seed)
    ks = jax.random.split(key, 32)

    def nrm(k, shape, scale):
        return jax.random.normal(k, shape, f32) * scale

    x = nrm(ks[0], (BATCH, SEQ, D_MODEL), 1.0)
    mem = nrm(ks[1], (BATCH, MEM_LEN, D_MODEL), 1.0)
    offset = jax.random.randint(ks[2], (BATCH, 1), 0, 4096, dtype=jnp.int32)
    positions = offset + jnp.arange(SEQ, dtype=jnp.int32)[None, :]
    return {
        "x": x,
        "mem": mem,
        "positions": positions,
        "even_w_in": nrm(ks[3], (N_EVEN, D_MODEL, EVEN_IN), D_MODEL ** -0.5),
        "even_w_out": nrm(ks[4], (N_EVEN, EVEN_MIX, D_MODEL), EVEN_MIX ** -0.5 * DEEPNORM_BETA),
        "gla_gate_w": nrm(ks[5], (N_EVEN, GLA_GATE_RANK, GLA_HEADS * GLA_DK), GLA_GATE_RANK ** -0.5),
        "gla_gate_b": nrm(ks[6], (N_EVEN, GLA_HEADS * GLA_DK), 0.1),
        "gla_norm_g": 1.0 + nrm(ks[7], (N_EVEN, GLA_DV), 0.02),
        "diff_lam_q1": nrm(ks[8], (N_EVEN, DIFF_DH), 0.1),
        "diff_lam_k1": nrm(ks[9], (N_EVEN, DIFF_DH), 0.1),
        "diff_lam_q2": nrm(ks[10], (N_EVEN, DIFF_DH), 0.1),
        "diff_lam_k2": nrm(ks[11], (N_EVEN, DIFF_DH), 0.1),
        "diff_norm_g": 1.0 + nrm(ks[12], (N_EVEN, DIFF_DV), 0.02),
        "odd_w_in": nrm(ks[13], (N_ODD, D_MODEL, ODD_IN), D_MODEL ** -0.5),
        "odd_w_out": nrm(ks[14], (N_ODD, ODD_MIX, D_MODEL), ODD_MIX ** -0.5 * DEEPNORM_BETA),
        "swa_sinks": nrm(ks[15], (N_ODD, SWA_Q_HEADS), 0.5),
        "mem_w_q": nrm(ks[16], (DEPTH, D_MODEL, D_MODEL), D_MODEL ** -0.5),
        "mem_w_kv": nrm(ks[17], (DEPTH, D_MODEL, 2 * D_MODEL), D_MODEL ** -0.5),
        "mem_w_out": nrm(ks[18], (DEPTH, D_MODEL, D_MODEL), D_MODEL ** -0.5 * DEEPNORM_BETA),
        "ffn_w_in": nrm(ks[19], (DEPTH, D_MODEL, 2 * D_FF), D_MODEL ** -0.5),
        "ffn_conv_w": nrm(ks[20], (DEPTH, CONV_WIDTH, D_FF), CONV_WIDTH ** -0.5),
        "ffn_conv_b": nrm(ks[21], (DEPTH, D_FF), 0.02),
        "ffn_w_out": nrm(ks[22], (DEPTH, D_FF, D_MODEL), D_FF ** -0.5 * DEEPNORM_BETA),
        "ln_g": 1.0 + nrm(ks[23], (DEPTH, 3, D_MODEL), 0.02),
        "ln_b": nrm(ks[24], (DEPTH, 3, D_MODEL), 0.02),
    }


def reference(x, mem, positions, even_w_in, even_w_out, gla_gate_w, gla_gate_b, gla_norm_g,
              diff_lam_q1, diff_lam_k1, diff_lam_q2, diff_lam_k2, diff_norm_g,
              odd_w_in, odd_w_out, swa_sinks, mem_w_q, mem_w_kv, mem_w_out,
              ffn_w_in, ffn_conv_w, ffn_conv_b, ffn_w_out, ln_g, ln_b):
    cos, sin = rope_tables(positions, ATTN_HEAD_DIM)
    for i in range(DEPTH):
        j = i // 2
        if i % 2 == 0:
            lam_init = 0.8 - 0.6 * float(np.exp(-0.3 * i))
            h = even_mixer(x, cos, sin, even_w_in[j], even_w_out[j], gla_gate_w[j], gla_gate_b[j],
                           gla_norm_g[j], diff_lam_q1[j], diff_lam_k1[j], diff_lam_q2[j], diff_lam_k2[j],
                           diff_norm_g[j], lam_init)
        else:
            h = odd_mixer(x, cos, sin, odd_w_in[j], odd_w_out[j], swa_sinks[j])
        x = layer_norm(DEEPNORM_ALPHA * x + h, ln_g[i, 0], ln_b[i, 0])
        h = memory_cross_attention(x, mem, mem_w_q[i], mem_w_kv[i], mem_w_out[i])
        x = layer_norm(DEEPNORM_ALPHA * x + h, ln_g[i, 1], ln_b[i, 1])
        h = conv_ffn(x, ffn_w_in[i], ffn_conv_w[i], ffn_conv_b[i], ffn_w_out[i])
        x = layer_norm(DEEPNORM_ALPHA * x + h, ln_g[i, 2], ln_b[i, 2])
    return x
```

```python
import functools

import numpy as np
import jax
import jax.numpy as jnp
from jax import lax
from jax.experimental import pallas as pl
from jax.experimental.pallas import tpu as pltpu

f32 = jnp.float32
bf16 = jnp.bfloat16

D_MODEL = 1024
DEPTH = 4
CHUNK = 64
ROPE_THETA = 10000.0
LN_EPS = 1e-5
RMS_EPS = 1e-6
NEG_INF = -1e30
HEAD_DIM = 64

GLA_HEADS = 4
GLA_DK = 64
GLA_DV = 128
GLA_GATE_RANK = 16
GLA_TAU = 16.0

DIFF_HEADS = 4
DIFF_DV = 128

SWA_Q_HEADS = 16
SWA_KV_HEADS = 2
SWA_GROUP = SWA_Q_HEADS // SWA_KV_HEADS
WINDOW_CHUNKS = 2

MEM_HEADS = 4
MEM_DH = D_MODEL // MEM_HEADS

D_FF = 2816
DEEPNORM_ALPHA = (2 * DEPTH) ** 0.25

LANES = 128
VMEM_LIMIT = 56 * 1024 * 1024

_NT = (((1,), (1,)), ((), ()))
_TN = (((0,), (0,)), ((), ()))


def _params(semantics):
    return pltpu.CompilerParams(dimension_semantics=semantics, vmem_limit_bytes=VMEM_LIMIT)


def _dot(a, b):
    return jnp.dot(a, b, preferred_element_type=f32)


def _deepnorm_ln(x, h, g, b):
    y = DEEPNORM_ALPHA * x + h
    mu = jnp.mean(y, axis=-1, keepdims=True)
    d = y - mu
    var = jnp.mean(d * d, axis=-1, keepdims=True)
    return d * lax.rsqrt(var + LN_EPS) * g + b


def _rope(y, cos_t, sin_t):
    rows = y.shape[0]
    lane = lax.broadcasted_iota(jnp.int32, (rows, LANES), 1)
    first_half = (lane & 32) == 0
    outs = []
    for j in range(y.shape[1] // LANES):
        t = y[:, j * LANES:(j + 1) * LANES]
        swapped = jnp.where(first_half, pltpu.roll(t, LANES - 32, 1), pltpu.roll(t, 32, 1))
        outs.append(t * cos_t + swapped * sin_t)
    return jnp.concatenate(outs, axis=1)


EVEN_F32_COLS = 1024
EVEN_BF_COLS = 2176


def _even_in_kernel(x_ref, w_ref, cos_ref, sin_ref, of_ref, ob_ref):
    xb = x_ref[...].astype(bf16)
    of_ref[...] = _dot(xb, w_ref[:, 0:1024])
    ob_ref[:, 0:512] = _dot(xb, w_ref[:, 1024:1536]).astype(bf16)
    c = cos_ref[...]
    s = sin_ref[...]
    ob_ref[:, 512:1024] = _rope(_dot(xb, w_ref[:, 1536:2048]), c, s).astype(bf16)
    ob_ref[:, 1024:1536] = _rope(_dot(xb, w_ref[:, 2048:2560]), c, s).astype(bf16)
    ob_ref[:, 1536:2176] = _dot(xb, w_ref[:, 2560:3200]).astype(bf16)


def _even_in_proj(x, w, cos_t, sin_t, tm):
    T = x.shape[0]
    return pl.pallas_call(
        _even_in_kernel,
        grid=(T // tm,),
        in_specs=[
            pl.BlockSpec((tm, D_MODEL), lambda i: (i, 0)),
            pl.BlockSpec(w.shape, lambda i: (0, 0)),
            pl.BlockSpec((tm, LANES), lambda i: (i, 0)),
            pl.BlockSpec((tm, LANES), lambda i: (i, 0)),
        ],
        out_specs=[
            pl.BlockSpec((tm, EVEN_F32_COLS), lambda i: (i, 0)),
            pl.BlockSpec((tm, EVEN_BF_COLS), lambda i: (i, 0)),
        ],
        out_shape=[
            jax.ShapeDtypeStruct((T, EVEN_F32_COLS), f32),
            jax.ShapeDtypeStruct((T, EVEN_BF_COLS), bf16),
        ],
        compiler_params=_params(("parallel",)),
        name="even_in_proj",
    )(x, w, cos_t, sin_t)


def _gla_kernel(of_ref, va_ref, ga_ref, gw_ref, gb_ref, ng_ref, o_ref, st_ref, *, nchunk):
    tc = nchunk * CHUNK
    hk = GLA_HEADS * GLA_DK

    @pl.when(pl.program_id(1) == 0)
    def _():
        st_ref[...] = jnp.zeros_like(st_ref)

    z = _dot(ga_ref[...], gw_ref[...]) + gb_ref[...]
    log_a = (jnp.minimum(z, 0.0) - jnp.log(1.0 + jnp.exp(-jnp.abs(z)))) * (1.0 / GLA_TAU)

    row = lax.broadcasted_iota(jnp.int32, (tc, tc), 0)
    col = lax.broadcasted_iota(jnp.int32, (tc, tc), 1)
    tri = jnp.where((col <= row) & ((col // CHUNK) == (row // CHUNK)), 1.0, 0.0).astype(bf16)
    hi = log_a.astype(bf16)
    rem = log_a - hi.astype(f32)
    mid = rem.astype(bf16)
    lo = (rem - mid.astype(f32)).astype(bf16)
    b = _dot(tri, hi) + _dot(tri, mid) + _dot(tri, lo)

    b3 = b.reshape(nchunk, CHUNK, hk)
    b_last = b3[:, CHUNK - 1:CHUNK, :]
    k_carry = jnp.exp(b_last - b3).reshape(tc, hk)
    decay = jnp.exp(b_last)

    qa = of_ref[:, 0:hk]
    ka = of_ref[:, hk:2 * hk]
    q_dec = (qa * jnp.exp(b)).astype(bf16)
    k_neg = (ka * jnp.exp(-b)).astype(bf16)
    k_dec = (ka * k_carry).astype(bf16)

    r = lax.broadcasted_iota(jnp.int32, (CHUNK, CHUNK), 0)
    cc = lax.broadcasted_iota(jnp.int32, (CHUNK, CHUNK), 1)
    causal = cc <= r
    gain = ng_ref[...]

    for h in range(GLA_HEADS):
        ks = slice(h * GLA_DK, (h + 1) * GLA_DK)
        vs = slice(h * GLA_DV, (h + 1) * GLA_DV)
        state_t = st_ref[h]
        for c in range(nchunk):
            rs = slice(c * CHUNK, (c + 1) * CHUNK)
            qd = q_dec[rs, ks]
            v = va_ref[rs, vs]
            a = lax.dot_general(qd, k_neg[rs, ks], _NT, preferred_element_type=f32)
            a = jnp.where(causal, a, 0.0).astype(bf16)
            o = _dot(a, v) + lax.dot_general(qd, state_t.astype(bf16), _NT, preferred_element_type=f32)
            kv_t = lax.dot_general(v, k_dec[rs, ks], _TN, preferred_element_type=f32)
            state_t = decay[c, :, ks] * state_t + kv_t
            o = o * lax.rsqrt(jnp.mean(o * o, axis=-1, keepdims=True) + RMS_EPS) * gain
            rr = of_ref[rs, 2 * hk + h * GLA_DV:2 * hk + (h + 1) * GLA_DV]
            o_ref[rs, vs] = (o * (rr * (1.0 / (1.0 + jnp.exp(-rr))))).astype(bf16)
        st_ref[h] = state_t


def _gla(of, ob, gate_w, gate_b, norm_g, B, S, nchunk):
    T = of.shape[0]
    tc = nchunk * CHUNK
    nt = S // tc
    return pl.pallas_call(
        functools.partial(_gla_kernel, nchunk=nchunk),
        grid=(B, nt),
        in_specs=[
            pl.BlockSpec((tc, EVEN_F32_COLS), lambda b, i: (b * nt + i, 0)),
            pl.BlockSpec((tc, 512), lambda b, i: (b * nt + i, 0)),
            pl.BlockSpec((tc, LANES), lambda b, i: (b * nt + i, 2048 // LANES)),
            pl.BlockSpec(gate_w.shape, lambda b, i: (0, 0)),
            pl.BlockSpec(gate_b.shape, lambda b, i: (0, 0)),
            pl.BlockSpec(norm_g.shape, lambda b, i: (0, 0)),
        ],
        out_specs=pl.BlockSpec((tc, GLA_HEADS * GLA_DV), lambda b, i: (b * nt + i, 0)),
        out_shape=jax.ShapeDtypeStruct((T, GLA_HEADS * GLA_DV), bf16),
        scratch_shapes=[pltpu.VMEM((GLA_HEADS, GLA_DV, GLA_DK), f32)],
        compiler_params=_params(("parallel", "arbitrary")),
        name="gla",
    )(of, ob, ob, gate_w, gate_b, norm_g)


def _diff_kernel(q_ref, k_ref, v_ref, lq1_ref, lk1_ref, lq2_ref, lk2_ref, ng_ref, o_ref, *, tq, lam_init):
    i = pl.program_id(2)
    q = q_ref[...]
    q_maps = (q[:, 0:HEAD_DIM], q[:, HEAD_DIM:2 * HEAD_DIM])

    row = lax.broadcasted_iota(jnp.int32, (tq, tq), 0)
    col = lax.broadcasted_iota(jnp.int32, (tq, tq), 1)
    visible = (col // CHUNK) <= (row // CHUNK)

    def tile(j, carry, masked):
        start = pl.multiple_of(j * tq, tq)
        k = k_ref[pl.ds(start, tq), :]
        v = v_ref[pl.ds(start, tq), :]
        out = []
        for t in range(2):
            m, l, acc = carry[3 * t:3 * t + 3]
            s = lax.dot_general(q_maps[t], k[:, t * HEAD_DIM:(t + 1) * HEAD_DIM], _NT,
                                preferred_element_type=f32)
            if masked:
                s = jnp.where(visible, s, NEG_INF)
            m_new = jnp.maximum(m, jnp.max(s, axis=1, keepdims=True))
            alpha = jnp.exp(m - m_new)
            p = jnp.exp(s - m_new)
            l = alpha * l + jnp.sum(p, axis=1, keepdims=True)
            acc = alpha * acc + _dot(p.astype(bf16), v)
            out += [m_new, l, acc]
        return tuple(out)

    init = (jnp.full((tq, 1), NEG_INF, f32), jnp.zeros((tq, 1), f32), jnp.zeros((tq, DIFF_DV), f32)) * 2
    carry = lax.fori_loop(0, i, lambda j, c: tile(j, c, False), init)
    m0, l0, acc0, m1, l1, acc1 = tile(i, carry, True)

    lam = (jnp.exp(jnp.sum(lq1_ref[...] * lk1_ref[...], axis=1, keepdims=True))
           - jnp.exp(jnp.sum(lq2_ref[...] * lk2_ref[...], axis=1, keepdims=True)) + lam_init)
    o = acc0 * (1.0 / l0) - lam * (acc1 * (1.0 / l1))
    o = o * lax.rsqrt(jnp.mean(o * o, axis=-1, keepdims=True) + RMS_EPS) * ng_ref[...]
    o_ref[...] = (o * (1.0 - lam_init)).astype(bf16)


def _diff_attention(ob, lq1, lk1, lq2, lk2, norm_g, lam_init, B, S, tq):
    T = ob.shape[0]
    nq = S // tq
    small = lambda a: pl.BlockSpec(a.shape, lambda b, h, i: (0, 0))
    return pl.pallas_call(
        functools.partial(_diff_kernel, tq=tq, lam_init=lam_init),
        grid=(B, DIFF_HEADS, nq),
        in_specs=[
            pl.BlockSpec((tq, LANES), lambda b, h, i: (b * nq + i, 4 + h)),
            pl.BlockSpec((S, LANES), lambda b, h, i: (b, 8 + h)),
            pl.BlockSpec((S, LANES), lambda b, h, i: (b, 12 + h)),
            small(lq1), small(lk1), small(lq2), small(lk2), small(norm_g),
        ],
        out_specs=pl.BlockSpec((tq, DIFF_DV), lambda b, h, i: (b * nq + i, h)),
        out_shape=jax.ShapeDtypeStruct((T, DIFF_HEADS * DIFF_DV), bf16),
        compiler_params=_params(("parallel", "parallel", "arbitrary")),
        name="diff_attention",
    )(ob, ob, ob, lq1, lk1, lq2, lk2, norm_g)


def _out_ln_kernel(*refs, n_act):
    acts = refs[:n_act]
    w_ref, x_ref, g_ref, b_ref, o_ref = refs[n_act:]
    h = None
    off = 0
    for a_ref in acts:
        k = a_ref.shape[1]
        part = _dot(a_ref[...], w_ref[off:off + k, :])
        h = part if h is None else h + part
        off += k
    o_ref[...] = _deepnorm_ln(x_ref[...], h, g_ref[...], b_ref[...])


def _out_proj_ln(acts, w, x, g, b, tm):
    T = x.shape[0]
    return pl.pallas_call(
        functools.partial(_out_ln_kernel, n_act=len(acts)),
        grid=(T // tm,),
        in_specs=[pl.BlockSpec((tm, a.shape[1]), lambda i: (i, 0)) for a in acts] + [
            pl.BlockSpec(w.shape, lambda i: (0, 0)),
            pl.BlockSpec((tm, D_MODEL), lambda i: (i, 0)),
            pl.BlockSpec((1, D_MODEL), lambda i: (0, 0)),
            pl.BlockSpec((1, D_MODEL), lambda i: (0, 0)),
        ],
        out_specs=pl.BlockSpec((tm, D_MODEL), lambda i: (i, 0)),
        out_shape=jax.ShapeDtypeStruct((T, D_MODEL), f32),
        compiler_params=_params(("parallel",)),
        name="out_proj_ln",
    )(*acts, w, x, g, b)


ODD_COLS = 1280


def _odd_in_kernel(x_ref, w_ref, cos_ref, sin_ref, o_ref):
    xb = x_ref[...].astype(bf16)
    c = cos_ref[...]
    s = sin_ref[...]
    o_ref[:, 0:1024] = _rope(_dot(xb, w_ref[:, 0:1024]), c, s).astype(bf16)
    kv = _dot(xb, w_ref[:, 1024:1280])
    o_ref[:, 1024:1152] = _rope(kv[:, 0:128], c, s).astype(bf16)
    o_ref[:, 1152:1280] = kv[:, 128:256].astype(bf16)


def _odd_in_proj(x, w, cos_t, sin_t, tm):
    T = x.shape[0]
    return pl.pallas_call(
        _odd_in_kernel,
        grid=(T // tm,),
        in_specs=[
            pl.BlockSpec((tm, D_MODEL), lambda i: (i, 0)),
            pl.BlockSpec(w.shape, lambda i: (0, 0)),
            pl.BlockSpec((tm, LANES), lambda i: (i, 0)),
            pl.BlockSpec((tm, LANES), lambda i: (i, 0)),
        ],
        out_specs=pl.BlockSpec((tm, ODD_COLS), lambda i: (i, 0)),
        out_shape=jax.ShapeDtypeStruct((T, ODD_COLS), bf16),
        compiler_params=_params(("parallel",)),
        name="odd_in_proj",
    )(x, w, cos_t, sin_t)


def _swa_kernel(q_ref, k_ref, v_ref, sink_ref, o_ref, *, nchunk):
    i = pl.program_id(1)
    band = (WINDOW_CHUNKS + 1) * CHUNK
    kidx = lax.broadcasted_iota(jnp.int32, (1, band), 1)
    for c in range(nchunk):
        n = i * nchunk + c
        start = pl.multiple_of(jnp.maximum(n - WINDOW_CHUNKS, 0) * CHUNK, CHUNK)
        visible = kidx < (jnp.minimum(n, WINDOW_CHUNKS) + 1) * CHUNK
        kb = k_ref[pl.ds(start, band), :]
        vb = v_ref[pl.ds(start, band), :]
        rs = slice(c * CHUNK, (c + 1) * CHUNK)
        for j in range(SWA_KV_HEADS):
            hs = slice(j * HEAD_DIM, (j + 1) * HEAD_DIM)
            qs = jnp.concatenate(
                [q_ref[rs, (j * SWA_GROUP + g) * HEAD_DIM:(j * SWA_GROUP + g + 1) * HEAD_DIM]
                 for g in range(SWA_GROUP)], axis=0)
            s = lax.dot_general(qs, kb[:, hs], _NT, preferred_element_type=f32)
            s = jnp.where(visible, s, NEG_INF)
            sink = sink_ref[j]
            m = jnp.maximum(jnp.max(s, axis=1, keepdims=True), sink)
            p = jnp.exp(s - m)
            den = jnp.sum(p, axis=1, keepdims=True) + jnp.exp(sink - m)
            o = _dot(p.astype(bf16), vb[:, hs]) * (1.0 / den)
            o_ref[rs, j * SWA_GROUP * HEAD_DIM:(j + 1) * SWA_GROUP * HEAD_DIM] = jnp.concatenate(
                [o[g * CHUNK:(g + 1) * CHUNK, :] for g in range(SWA_GROUP)], axis=1).astype(bf16)


def _swa(qkv, sink_rows, B, S, nchunk):
    T = qkv.shape[0]
    tq = nchunk * CHUNK
    nt = S // tq
    return pl.pallas_call(
        functools.partial(_swa_kernel, nchunk=nchunk),
        grid=(B, nt),
        in_specs=[
            pl.BlockSpec((tq, 1024), lambda b, i: (b * nt + i, 0)),
            pl.BlockSpec((S, LANES), lambda b, i: (b, 8)),
            pl.BlockSpec((S, LANES), lambda b, i: (b, 9)),
            pl.BlockSpec(sink_rows.shape, lambda b, i: (0, 0, 0)),
        ],
        out_specs=pl.BlockSpec((tq, 1024), lambda b, i: (b * nt + i, 0)),
        out_shape=jax.ShapeDtypeStruct((T, 1024), bf16),
        compiler_params=_params(("parallel", "arbitrary")),
        name="swa",
    )(qkv, qkv, qkv, sink_rows)


def _matmul_kernel(a_ref, w_ref, o_ref):
    o_ref[...] = _dot(a_ref[...].astype(bf16), w_ref[...]).astype(o_ref.dtype)


def _matmul(a, w, tm, out_dtype):
    M, K = a.shape
    N = w.shape[1]
    return pl.pallas_call(
        _matmul_kernel,
        grid=(M // tm,),
        in_specs=[pl.BlockSpec((tm, K), lambda i: (i, 0)), pl.BlockSpec((K, N), lambda i: (0, 0))],
        out_specs=pl.BlockSpec((tm, N), lambda i: (i, 0)),
        out_shape=jax.ShapeDtypeStruct((M, N), out_dtype),
        compiler_params=_params(("parallel",)),
        name="matmul",
    )(a, w)


def _mem_attn_kernel(x_ref, wq_ref, kv_ref, wo_ref, g_ref, b_ref, o_ref):
    x = x_ref[...]
    q = _dot(x.astype(bf16), wq_ref[...]).astype(bf16)
    heads = []
    for h in range(MEM_HEADS):
        hs = slice(h * MEM_DH, (h + 1) * MEM_DH)
        s = lax.dot_general(q[:, hs], kv_ref[:, hs], _NT, preferred_element_type=f32)
        p = jnp.exp(s - jnp.max(s, axis=1, keepdims=True))
        inv = 1.0 / jnp.sum(p, axis=1, keepdims=True)
        o = _dot(p.astype(bf16), kv_ref[:, D_MODEL + h * MEM_DH:D_MODEL + (h + 1) * MEM_DH]) * inv
        heads.append(o.astype(bf16))
    o = jnp.concatenate(heads, axis=1)
    o_ref[...] = _deepnorm_ln(x, _dot(o, wo_ref[...]), g_ref[...], b_ref[...])


def _mem_attention(x, wq, kv, wo, g, b, B, S, tm):
    T = x.shape[0]
    nt = S // tm
    M = kv.shape[0] // B
    return pl.pallas_call(
        _mem_attn_kernel,
        grid=(B, nt),
        in_specs=[
            pl.BlockSpec((tm, D_MODEL), lambda bb, i: (bb * nt + i, 0)),
            pl.BlockSpec(wq.shape, lambda bb, i: (0, 0)),
            pl.BlockSpec((M, 2 * D_MODEL), lambda bb, i: (bb, 0)),
            pl.BlockSpec(wo.shape, lambda bb, i: (0, 0)),
            pl.BlockSpec((1, D_MODEL), lambda bb, i: (0, 0)),
            pl.BlockSpec((1, D_MODEL), lambda bb, i: (0, 0)),
        ],
        out_specs=pl.BlockSpec((tm, D_MODEL), lambda bb, i: (bb * nt + i, 0)),
        out_shape=jax.ShapeDtypeStruct((T, D_MODEL), f32),
        compiler_params=_params(("parallel", "parallel")),
        name="mem_attention",
    )(x, wq, kv, wo, g, b)


FFN_COL_TILE = 256
HALO_ROWS = 8


def _ffn_kernel(x_ref, xp_ref, wi_ref, cw_ref, cb_ref, wo_ref, g_ref, b_ref, o_ref, h_ref, *, tiles_per_seq):
    tm = x_ref.shape[0]
    x = x_ref[...]
    xb = x.astype(bf16)
    has_prev = (pl.program_id(0) % tiles_per_seq) != 0
    xpb = jnp.where(has_prev, xp_ref[...], 0.0).astype(bf16)
    row = lax.broadcasted_iota(jnp.int32, (tm, FFN_COL_TILE), 0)
    for f in range(D_FF // FFN_COL_TILE):
        cs = slice(f * FFN_COL_TILE, (f + 1) * FFN_COL_TILE)
        g = _dot(xb, wi_ref[:, cs])
        u = _dot(xb, wi_ref[:, D_FF + f * FFN_COL_TILE:D_FF + (f + 1) * FFN_COL_TILE])
        gp = _dot(xpb, wi_ref[:, cs])
        g1 = jnp.where(row < 1, gp[HALO_ROWS - 1:HALO_ROWS, :], pltpu.roll(g, 1, 0))
        g2 = jnp.where(row < 2, jnp.where(row < 1, gp[HALO_ROWS - 2:HALO_ROWS - 1, :],
                                          gp[HALO_ROWS - 1:HALO_ROWS, :]), pltpu.roll(g, 2, 0))
        y = cw_ref[0:1, cs] * g2 + cw_ref[1:2, cs] * g1 + cw_ref[2:3, cs] * g + cb_ref[:, cs]
        cdf = 0.5 * (1.0 + jnp.tanh(np.sqrt(2.0 / np.pi).astype(np.float32) * (y + 0.044715 * (y * y * y))))
        h_ref[:, cs] = (y * cdf * u).astype(bf16)
    o_ref[...] = _deepnorm_ln(x, _dot(h_ref[...], wo_ref[...]), g_ref[...], b_ref[...])


def _ffn(x, wi, cw, cb, wo, g, b, S, tm):
    T = x.shape[0]
    tiles_per_seq = S // tm
    halo_blocks = tm // HALO_ROWS
    single = pl.Buffered(1)
    return pl.pallas_call(
        functools.partial(_ffn_kernel, tiles_per_seq=tiles_per_seq),
        grid=(T // tm,),
        in_specs=[
            pl.BlockSpec((tm, D_MODEL), lambda i: (i, 0)),
            pl.BlockSpec((HALO_ROWS, D_MODEL), lambda i: (jnp.maximum(i * halo_blocks - 1, 0), 0)),
            pl.BlockSpec(wi.shape, lambda i: (0, 0), pipeline_mode=single),
            pl.BlockSpec(cw.shape, lambda i: (0, 0)),
            pl.BlockSpec(cb.shape, lambda i: (0, 0)),
            pl.BlockSpec(wo.shape, lambda i: (0, 0), pipeline_mode=single),
            pl.BlockSpec((1, D_MODEL), lambda i: (0, 0)),
            pl.BlockSpec((1, D_MODEL), lambda i: (0, 0)),
        ],
        out_specs=pl.BlockSpec((tm, D_MODEL), lambda i: (i, 0)),
        out_shape=jax.ShapeDtypeStruct((T, D_MODEL), f32),
        scratch_shapes=[pltpu.VMEM((tm, D_FF), bf16)],
        compiler_params=_params(("parallel",)),
        name="conv_ffn",
    )(x, x, wi, cw, cb, wo, g, b)


def _even_weight(w_in):
    qa, ka, va, ra, ga, qb, kb, vb = jnp.split(
        w_in, np.cumsum([256, 256, 512, 512, GLA_GATE_RANK, 512, 512, 512])[:-1].tolist(), axis=1)
    pad = jnp.zeros((D_MODEL, LANES - GLA_GATE_RANK), w_in.dtype)
    return jnp.concatenate(
        [qa * GLA_DK ** -0.5, ka, ra, va, qb * HEAD_DIM ** -0.5, kb, vb, ga, pad], axis=1).astype(bf16)


def _odd_weight(w_in):
    return jnp.concatenate([w_in[:, :1024] * HEAD_DIM ** -0.5, w_in[:, 1024:]], axis=1).astype(bf16)


def kernel(x, mem, positions, even_w_in, even_w_out, gla_gate_w, gla_gate_b, gla_norm_g, diff_lam_q1, diff_lam_k1, diff_lam_q2, diff_lam_k2, diff_norm_g, odd_w_in, odd_w_out, swa_sinks, mem_w_q, mem_w_kv, mem_w_out, ffn_w_in, ffn_conv_w, ffn_conv_b, ffn_w_out, ln_g, ln_b):
    B, S, D = x.shape
    T = B * S
    tm = min(512, S)
    xf = x.reshape(T, D)

    inv_freq = ROPE_THETA ** (-jnp.arange(0, HEAD_DIM, 2, dtype=f32) / HEAD_DIM)
    ang = positions.astype(f32).reshape(T, 1) * inv_freq
    cos, sin = jnp.cos(ang), jnp.sin(ang)
    cos_t = jnp.tile(cos, (1, 4))
    sin_t = jnp.concatenate([-sin, sin, -sin, sin], axis=1)

    memf = mem.reshape(B * mem.shape[1], D)
    row2 = lambda v: v.reshape(1, -1)

    for i in range(DEPTH):
        j = i // 2
        if i % 2 == 0:
            lam_init = 0.8 - 0.6 * float(np.exp(-0.3 * i))
            of, ob = _even_in_proj(xf, _even_weight(even_w_in[j]), cos_t, sin_t, tm)
            gate_w = jnp.pad(gla_gate_w[j], ((0, LANES - GLA_GATE_RANK), (0, 0))).astype(bf16)
            o_a = _gla(of, ob, gate_w, row2(gla_gate_b[j]), row2(gla_norm_g[j]), B, S, min(4, S // CHUNK))
            o_b = _diff_attention(ob, row2(diff_lam_q1[j]), row2(diff_lam_k1[j]), row2(diff_lam_q2[j]),
                                  row2(diff_lam_k2[j]), row2(diff_norm_g[j]), lam_init, B, S, tm)
            xf = _out_proj_ln([o_a, o_b], even_w_out[j].astype(bf16), xf, row2(ln_g[i, 0]), row2(ln_b[i, 0]), tm)
        else:
            qkv = _odd_in_proj(xf, _odd_weight(odd_w_in[j]), cos_t, sin_t, tm)
            sink_rows = jnp.repeat(swa_sinks[j].reshape(SWA_KV_HEADS, SWA_GROUP), CHUNK, axis=1)[..., None]
            o_c = _swa(qkv, sink_rows, B, S, min(8, S // CHUNK))
            xf = _out_proj_ln([o_c], odd_w_out[j].astype(bf16), xf, row2(ln_g[i, 0]), row2(ln_b[i, 0]), tm)

        kv = _matmul(memf, mem_w_kv[i].astype(bf16), memf.shape[0], bf16)
        xf = _mem_attention(xf, (mem_w_q[i] * MEM_DH ** -0.5).astype(bf16), kv, mem_w_out[i].astype(bf16),
                            row2(ln_g[i, 1]), row2(ln_b[i, 1]), B, S, tm)
        xf = _ffn(xf, ffn_w_in[i].astype(bf16), ffn_conv_w[i], row2(ffn_conv_b[i]), ffn_w_out[i].astype(bf16),
                  row2(ln_g[i, 2]), row2(ln_b[i, 2]), S, tm)
    return xf.reshape(B, S, D)
```

```python
import functools

import numpy as np
import jax
import jax.numpy as jnp
from jax import lax
from jax.experimental import pallas as pl
from jax.experimental.pallas import tpu as pltpu

f32 = jnp.float32
bf16 = jnp.bfloat16

D_MODEL = 1024
DEPTH = 4
CHUNK = 64
ROPE_THETA = 10000.0
LN_EPS = 1e-5
RMS_EPS = 1e-6
NEG_INF = -1e30
HEAD_DIM = 64

GLA_HEADS = 4
GLA_DK = 64
GLA_DV = 128
GLA_GATE_RANK = 16
GLA_TAU = 16.0

DIFF_HEADS = 4
DIFF_DV = 128

SWA_Q_HEADS = 16
SWA_KV_HEADS = 2
SWA_GROUP = SWA_Q_HEADS // SWA_KV_HEADS
WINDOW_CHUNKS = 2

MEM_HEADS = 4
MEM_DH = D_MODEL // MEM_HEADS

D_FF = 2816
DEEPNORM_ALPHA = (2 * DEPTH) ** 0.25

LANES = 128
VMEM_LIMIT = 56 * 1024 * 1024

_NT = (((1,), (1,)), ((), ()))
_TN = (((0,), (0,)), ((), ()))


def _params(semantics):
    return pltpu.CompilerParams(dimension_semantics=semantics, vmem_limit_bytes=VMEM_LIMIT)


def _dot(a, b):
    return jnp.dot(a, b, preferred_element_type=f32)


def _deepnorm_ln(x, h, g, b):
    y = DEEPNORM_ALPHA * x + h
    mu = jnp.mean(y, axis=-1, keepdims=True)
    d = y - mu
    var = jnp.mean(d * d, axis=-1, keepdims=True)
    return d * lax.rsqrt(var + LN_EPS) * g + b


def _rope(y, cos_t, sin_t):
    rows = y.shape[0]
    lane = lax.broadcasted_iota(jnp.int32, (rows, LANES), 1)
    first_half = (lane & 32) == 0
    outs = []
    for j in range(y.shape[1] // LANES):
        t = y[:, j * LANES:(j + 1) * LANES]
        swapped = jnp.where(first_half, pltpu.roll(t, LANES - 32, 1), pltpu.roll(t, 32, 1))
        outs.append(t * cos_t + swapped * sin_t)
    return jnp.concatenate(outs, axis=1)


EVEN_F32_COLS = 1024
EVEN_BF_COLS = 1664
DIFF_V_COLS = DIFF_HEADS * DIFF_DV


def _even_in_kernel(x_ref, w_ref, cos_ref, sin_ref, of_ref, ob_ref, vt_ref):
    xb = x_ref[...].astype(bf16)
    of_ref[...] = _dot(xb, w_ref[:, 0:1024])
    ob_ref[:, 0:512] = _dot(xb, w_ref[:, 1024:1536]).astype(bf16)
    c = cos_ref[...]
    s = sin_ref[...]
    ob_ref[:, 512:1024] = _rope(_dot(xb, w_ref[:, 1536:2048]), c, s).astype(bf16)
    ob_ref[:, 1024:1536] = _rope(_dot(xb, w_ref[:, 2048:2560]), c, s).astype(bf16)
    ob_ref[:, 1536:1664] = _dot(xb, w_ref[:, 2560:2688]).astype(bf16)
    vt_ref[...] = _dot(xb, w_ref[:, 2688:3200]).T.astype(bf16)


def _even_in_proj(x, w, cos_t, sin_t, B, S, tm):
    T = x.shape[0]
    nt = S // tm
    return pl.pallas_call(
        _even_in_kernel,
        grid=(T // tm,),
        in_specs=[
            pl.BlockSpec((tm, D_MODEL), lambda i: (i, 0)),
            pl.BlockSpec(w.shape, lambda i: (0, 0)),
            pl.BlockSpec((tm, LANES), lambda i: (i, 0)),
            pl.BlockSpec((tm, LANES), lambda i: (i, 0)),
        ],
        out_specs=[
            pl.BlockSpec((tm, EVEN_F32_COLS), lambda i: (i, 0)),
            pl.BlockSpec((tm, EVEN_BF_COLS), lambda i: (i, 0)),
            pl.BlockSpec((DIFF_V_COLS, tm), lambda i: (i // nt, i % nt)),
        ],
        out_shape=[
            jax.ShapeDtypeStruct((T, EVEN_F32_COLS), f32),
            jax.ShapeDtypeStruct((T, EVEN_BF_COLS), bf16),
            jax.ShapeDtypeStruct((B * DIFF_V_COLS, S), bf16),
        ],
        compiler_params=_params(("parallel",)),
        name="even_in_proj",
    )(x, w, cos_t, sin_t)


def _gla_kernel(of_ref, va_ref, ga_ref, gw_ref, gb_ref, ng_ref, o_ref, st_ref, *, nchunk):
    tc = nchunk * CHUNK
    hk = GLA_HEADS * GLA_DK

    @pl.when(pl.program_id(1) == 0)
    def _():
        st_ref[...] = jnp.zeros_like(st_ref)

    z = _dot(ga_ref[...], gw_ref[...]) + gb_ref[...]
    log_a = (jnp.minimum(z, 0.0) - jnp.log(1.0 + jnp.exp(-jnp.abs(z)))) * (1.0 / GLA_TAU)

    row = lax.broadcasted_iota(jnp.int32, (tc, tc), 0)
    col = lax.broadcasted_iota(jnp.int32, (tc, tc), 1)
    tri = jnp.where((col <= row) & ((col // CHUNK) == (row // CHUNK)), 1.0, 0.0).astype(bf16)
    hi = log_a.astype(bf16)
    rem = log_a - hi.astype(f32)
    mid = rem.astype(bf16)
    lo = (rem - mid.astype(f32)).astype(bf16)
    b = _dot(tri, hi) + _dot(tri, mid) + _dot(tri, lo)

    b3 = b.reshape(nchunk, CHUNK, hk)
    b_last = b3[:, CHUNK - 1:CHUNK, :]
    k_carry = jnp.exp(b_last - b3).reshape(tc, hk)
    decay = jnp.exp(b_last)

    qa = of_ref[:, 0:hk]
    ka = of_ref[:, hk:2 * hk]
    q_dec = (qa * jnp.exp(b)).astype(bf16)
    k_neg = (ka * jnp.exp(-b)).astype(bf16)
    k_dec = (ka * k_carry).astype(bf16)

    r = lax.broadcasted_iota(jnp.int32, (CHUNK, CHUNK), 0)
    cc = lax.broadcasted_iota(jnp.int32, (CHUNK, CHUNK), 1)
    causal = cc <= r
    gain = ng_ref[...]

    for h in range(GLA_HEADS):
        ks = slice(h * GLA_DK, (h + 1) * GLA_DK)
        vs = slice(h * GLA_DV, (h + 1) * GLA_DV)
        state_t = st_ref[h]
        for c in range(nchunk):
            rs = slice(c * CHUNK, (c + 1) * CHUNK)
            qd = q_dec[rs, ks]
            v = va_ref[rs, vs]
            a = lax.dot_general(qd, k_neg[rs, ks], _NT, preferred_element_type=f32)
            a = jnp.where(causal, a, 0.0).astype(bf16)
            o = _dot(a, v) + lax.dot_general(qd, state_t.astype(bf16), _NT, preferred_element_type=f32)
            kv_t = lax.dot_general(v, k_dec[rs, ks], _TN, preferred_element_type=f32)
            state_t = decay[c, :, ks] * state_t + kv_t
            o = o * lax.rsqrt(jnp.mean(o * o, axis=-1, keepdims=True) + RMS_EPS) * gain
            rr = of_ref[rs, 2 * hk + h * GLA_DV:2 * hk + (h + 1) * GLA_DV]
            o_ref[rs, vs] = (o * (rr * (1.0 / (1.0 + jnp.exp(-rr))))).astype(bf16)
        st_ref[h] = state_t


def _gla(of, ob, gate_w, gate_b, norm_g, B, S, nchunk):
    T = of.shape[0]
    tc = nchunk * CHUNK
    nt = S // tc
    return pl.pallas_call(
        functools.partial(_gla_kernel, nchunk=nchunk),
        grid=(B, nt),
        in_specs=[
            pl.BlockSpec((tc, EVEN_F32_COLS), lambda b, i: (b * nt + i, 0)),
            pl.BlockSpec((tc, 512), lambda b, i: (b * nt + i, 0)),
            pl.BlockSpec((tc, LANES), lambda b, i: (b * nt + i, 1536 // LANES)),
            pl.BlockSpec(gate_w.shape, lambda b, i: (0, 0)),
            pl.BlockSpec(gate_b.shape, lambda b, i: (0, 0)),
            pl.BlockSpec(norm_g.shape, lambda b, i: (0, 0)),
        ],
        out_specs=pl.BlockSpec((tc, GLA_HEADS * GLA_DV), lambda b, i: (b * nt + i, 0)),
        out_shape=jax.ShapeDtypeStruct((T, GLA_HEADS * GLA_DV), bf16),
        scratch_shapes=[pltpu.VMEM((GLA_HEADS, GLA_DV, GLA_DK), f32)],
        compiler_params=_params(("parallel", "arbitrary")),
        name="gla",
    )(of, ob, ob, gate_w, gate_b, norm_g)


def _diff_kernel(q_ref, k_ref, vt_ref, lq1_ref, lk1_ref, lq2_ref, lk2_ref, ng_ref, o_ref, s_ref, *, tq,
                 lam_init):
    i = pl.program_id(2)
    q = q_ref[...]
    q_maps = (q[:, 0:HEAD_DIM], q[:, HEAD_DIM:2 * HEAD_DIM])

    key = lax.broadcasted_iota(jnp.int32, (tq, tq), 0)
    qry = lax.broadcasted_iota(jnp.int32, (tq, tq), 1)
    visible = (key // CHUNK) <= (qry // CHUNK)

    def produce(j, slot):
        start = pl.multiple_of(j * tq, tq)
        k = k_ref[pl.ds(start, tq), :]
        col_max = []
        for t in range(2):
            s = lax.dot_general(k[:, t * HEAD_DIM:(t + 1) * HEAD_DIM], q_maps[t], _NT,
                                preferred_element_type=f32)
            s_ref[slot, t] = s
            col_max.append(jnp.max(s, axis=0, keepdims=True))
        return tuple(col_max)

    def consume(j, slot, col_max, state, masked):
        start = pl.multiple_of(j * tq, tq)
        vt = vt_ref[:, pl.ds(start, tq)]
        out = []
        for t in range(2):
            m, l, acc = state[3 * t:3 * t + 3]
            s = s_ref[slot, t]
            if masked:
                s = jnp.where(visible, s, NEG_INF)
                tile_max = jnp.max(s, axis=0, keepdims=True)
            else:
                tile_max = col_max[t]
            m_new = jnp.maximum(m, tile_max)
            alpha = jnp.exp(m - m_new)
            p = jnp.exp(s - m_new)
            l = alpha * l + jnp.sum(p, axis=0, keepdims=True)
            acc = alpha * acc + _dot(vt, p.astype(bf16))
            out += [m_new, l, acc]
        return tuple(out)

    def finish(state):
        m0, l0, acc0, m1, l1, acc1 = state
        lam = (jnp.exp(jnp.sum(lq1_ref[...] * lk1_ref[...], axis=1, keepdims=True))
               - jnp.exp(jnp.sum(lq2_ref[...] * lk2_ref[...], axis=1, keepdims=True)) + lam_init)
        o = acc0 * (1.0 / l0) - lam * (acc1 * (1.0 / l1))
        o = o * lax.rsqrt(jnp.mean(o * o, axis=0, keepdims=True) + RMS_EPS) * ng_ref[...]
        o_ref[...] = (o * (1.0 - lam_init)).T.astype(bf16)

    def pair(jj, carry):
        j = 2 * jj
        max1 = produce(j + 1, 1)
        state = consume(j, 0, carry[6:], carry[:6], False)
        max0 = produce(j + 2, 0)
        state = consume(j + 1, 1, max1, state, False)
        return state + max0

    init = (jnp.full((1, tq), NEG_INF, f32), jnp.zeros((1, tq), f32), jnp.zeros((DIFF_DV, tq), f32)) * 2
    carry = lax.fori_loop(0, i // 2, pair, init + produce(0, 0))
    state, col_max = carry[:6], carry[6:]

    @pl.when(i % 2 == 0)
    def _():
        finish(consume(i, 0, None, state, True))

    @pl.when(i % 2 == 1)
    def _():
        produce(i, 1)
        finish(consume(i, 1, None, consume(i - 1, 0, col_max, state, False), True))


def _diff_attention(ob, vt, lq1, lk1, lq2, lk2, norm_g, lam_init, B, S, tq):
    T = ob.shape[0]
    nq = S // tq
    small = lambda a: pl.BlockSpec(a.shape, lambda b, h, i: (0, 0))
    return pl.pallas_call(
        functools.partial(_diff_kernel, tq=tq, lam_init=lam_init),
        grid=(B, DIFF_HEADS, nq),
        in_specs=[
            pl.BlockSpec((tq, LANES), lambda b, h, i: (b * nq + i, 4 + h)),
            pl.BlockSpec((S, LANES), lambda b, h, i: (b, 8 + h)),
            pl.BlockSpec((DIFF_DV, S), lambda b, h, i: (b * DIFF_HEADS + h, 0)),
            small(lq1), small(lk1), small(lq2), small(lk2), small(norm_g),
        ],
        out_specs=pl.BlockSpec((tq, DIFF_DV), lambda b, h, i: (b * nq + i, h)),
        out_shape=jax.ShapeDtypeStruct((T, DIFF_HEADS * DIFF_DV), bf16),
        scratch_shapes=[pltpu.VMEM((2, 2, tq, tq), f32)],
        compiler_params=_params(("parallel", "parallel", "arbitrary")),
        name="diff_attention",
    )(ob, ob, vt, lq1, lk1, lq2, lk2, norm_g)


def _out_ln_kernel(*refs, n_act):
    acts = refs[:n_act]
    w_ref, x_ref, g_ref, b_ref, o_ref = refs[n_act:]
    h = None
    off = 0
    for a_ref in acts:
        k = a_ref.shape[1]
        part = _dot(a_ref[...], w_ref[off:off + k, :])
        h = part if h is None else h + part
        off += k
    o_ref[...] = _deepnorm_ln(x_ref[...], h, g_ref[...], b_ref[...])


def _out_proj_ln(acts, w, x, g, b, tm):
    T = x.shape[0]
    return pl.pallas_call(
        functools.partial(_out_ln_kernel, n_act=len(acts)),
        grid=(T // tm,),
        in_specs=[pl.BlockSpec((tm, a.shape[1]), lambda i: (i, 0)) for a in acts] + [
            pl.BlockSpec(w.shape, lambda i: (0, 0)),
            pl.BlockSpec((tm, D_MODEL), lambda i: (i, 0)),
            pl.BlockSpec((1, D_MODEL), lambda i: (0, 0)),
            pl.BlockSpec((1, D_MODEL), lambda i: (0, 0)),
        ],
        out_specs=pl.BlockSpec((tm, D_MODEL), lambda i: (i, 0)),
        out_shape=jax.ShapeDtypeStruct((T, D_MODEL), f32),
        compiler_params=_params(("parallel",)),
        name="out_proj_ln",
    )(*acts, w, x, g, b)


ODD_COLS = 1280


def _odd_in_kernel(x_ref, w_ref, cos_ref, sin_ref, o_ref):
    xb = x_ref[...].astype(bf16)
    c = cos_ref[...]
    s = sin_ref[...]
    o_ref[:, 0:1024] = _rope(_dot(xb, w_ref[:, 0:1024]), c, s).astype(bf16)
    kv = _dot(xb, w_ref[:, 1024:1280])
    o_ref[:, 1024:1152] = _rope(kv[:, 0:128], c, s).astype(bf16)
    o_ref[:, 1152:1280] = kv[:, 128:256].astype(bf16)


def _odd_in_proj(x, w, cos_t, sin_t, tm):
    T = x.shape[0]
    return pl.pallas_call(
        _odd_in_kernel,
        grid=(T // tm,),
        in_specs=[
            pl.BlockSpec((tm, D_MODEL), lambda i: (i, 0)),
            pl.BlockSpec(w.shape, lambda i: (0, 0)),
            pl.BlockSpec((tm, LANES), lambda i: (i, 0)),
            pl.BlockSpec((tm, LANES), lambda i: (i, 0)),
        ],
        out_specs=pl.BlockSpec((tm, ODD_COLS), lambda i: (i, 0)),
        out_shape=jax.ShapeDtypeStruct((T, ODD_COLS), bf16),
        compiler_params=_params(("parallel",)),
        name="odd_in_proj",
    )(x, w, cos_t, sin_t)


def _swa_kernel(q_ref, k_ref, v_ref, sink_ref, o_ref, *, nchunk):
    i = pl.program_id(1)
    band = (WINDOW_CHUNKS + 1) * CHUNK
    kidx = lax.broadcasted_iota(jnp.int32, (1, band), 1)
    for c in range(nchunk):
        n = i * nchunk + c
        start = pl.multiple_of(jnp.maximum(n - WINDOW_CHUNKS, 0) * CHUNK, CHUNK)
        visible = kidx < (jnp.minimum(n, WINDOW_CHUNKS) + 1) * CHUNK
        kb = k_ref[pl.ds(start, band), :]
        vb = v_ref[pl.ds(start, band), :]
        rs = slice(c * CHUNK, (c + 1) * CHUNK)
        for j in range(SWA_KV_HEADS):
            hs = slice(j * HEAD_DIM, (j + 1) * HEAD_DIM)
            qs = jnp.concatenate(
                [q_ref[rs, (j * SWA_GROUP + g) * HEAD_DIM:(j * SWA_GROUP + g + 1) * HEAD_DIM]
                 for g in range(SWA_GROUP)], axis=0)
            s = lax.dot_general(qs, kb[:, hs], _NT, preferred_element_type=f32)
            s = jnp.where(visible, s, NEG_INF)
            sink = sink_ref[j]
            m = jnp.maximum(jnp.max(s, axis=1, keepdims=True), sink)
            p = jnp.exp(s - m)
            den = jnp.sum(p, axis=1, keepdims=True) + jnp.exp(sink - m)
            o = _dot(p.astype(bf16), vb[:, hs]) * (1.0 / den)
            o_ref[rs, j * SWA_GROUP * HEAD_DIM:(j + 1) * SWA_GROUP * HEAD_DIM] = jnp.concatenate(
                [o[g * CHUNK:(g + 1) * CHUNK, :] for g in range(SWA_GROUP)], axis=1).astype(bf16)


def _swa(qkv, sink_rows, B, S, nchunk):
    T = qkv.shape[0]
    tq = nchunk * CHUNK
    nt = S // tq
    return pl.pallas_call(
        functools.partial(_swa_kernel, nchunk=nchunk),
        grid=(B, nt),
        in_specs=[
            pl.BlockSpec((tq, 1024), lambda b, i: (b * nt + i, 0)),
            pl.BlockSpec((S, LANES), lambda b, i: (b, 8)),
            pl.BlockSpec((S, LANES), lambda b, i: (b, 9)),
            pl.BlockSpec(sink_rows.shape, lambda b, i: (0, 0, 0)),
        ],
        out_specs=pl.BlockSpec((tq, 1024), lambda b, i: (b * nt + i, 0)),
        out_shape=jax.ShapeDtypeStruct((T, 1024), bf16),
        compiler_params=_params(("parallel", "arbitrary")),
        name="swa",
    )(qkv, qkv, qkv, sink_rows)


def _matmul_kernel(a_ref, w_ref, o_ref):
    o_ref[...] = _dot(a_ref[...].astype(bf16), w_ref[...]).astype(o_ref.dtype)


def _matmul(a, w, tm, out_dtype):
    M, K = a.shape
    N = w.shape[1]
    return pl.pallas_call(
        _matmul_kernel,
        grid=(M // tm,),
        in_specs=[pl.BlockSpec((tm, K), lambda i: (i, 0)), pl.BlockSpec((K, N), lambda i: (0, 0))],
        out_specs=pl.BlockSpec((tm, N), lambda i: (i, 0)),
        out_shape=jax.ShapeDtypeStruct((M, N), out_dtype),
        compiler_params=_params(("parallel",)),
        name="matmul",
    )(a, w)


def _mem_attn_kernel(x_ref, wq_ref, kv_ref, wo_ref, g_ref, b_ref, o_ref):
    x = x_ref[...]
    q = _dot(x.astype(bf16), wq_ref[...]).astype(bf16)
    heads = []
    for h in range(MEM_HEADS):
        hs = slice(h * MEM_DH, (h + 1) * MEM_DH)
        s = lax.dot_general(q[:, hs], kv_ref[:, hs], _NT, preferred_element_type=f32)
        p = jnp.exp(s - jnp.max(s, axis=1, keepdims=True))
        inv = 1.0 / jnp.sum(p, axis=1, keepdims=True)
        o = _dot(p.astype(bf16), kv_ref[:, D_MODEL + h * MEM_DH:D_MODEL + (h + 1) * MEM_DH]) * inv
        heads.append(o.astype(bf16))
    o = jnp.concatenate(heads, axis=1)
    o_ref[...] = _deepnorm_ln(x, _dot(o, wo_ref[...]), g_ref[...], b_ref[...])


def _mem_attention(x, wq, kv, wo, g, b, B, S, tm):
    T = x.shape[0]
    nt = S // tm
    M = kv.shape[0] // B
    return pl.pallas_call(
        _mem_attn_kernel,
        grid=(B, nt),
        in_specs=[
            pl.BlockSpec((tm, D_MODEL), lambda bb, i: (bb * nt + i, 0)),
            pl.BlockSpec(wq.shape, lambda bb, i: (0, 0)),
            pl.BlockSpec((M, 2 * D_MODEL), lambda bb, i: (bb, 0)),
            pl.BlockSpec(wo.shape, lambda bb, i: (0, 0)),
            pl.BlockSpec((1, D_MODEL), lambda bb, i: (0, 0)),
            pl.BlockSpec((1, D_MODEL), lambda bb, i: (0, 0)),
        ],
        out_specs=pl.BlockSpec((tm, D_MODEL), lambda bb, i: (bb * nt + i, 0)),
        out_shape=jax.ShapeDtypeStruct((T, D_MODEL), f32),
        compiler_params=_params(("parallel", "parallel")),
        name="mem_attention",
    )(x, wq, kv, wo, g, b)


FFN_COL_TILE = 256
HALO_ROWS = 8


def _ffn_kernel(x_ref, xp_ref, wi_ref, cw_ref, cb_ref, wo_ref, g_ref, b_ref, o_ref, h_ref, *, tiles_per_seq):
    tm = x_ref.shape[0]
    x = x_ref[...]
    xb = x.astype(bf16)
    has_prev = (pl.program_id(0) % tiles_per_seq) != 0
    xpb = jnp.where(has_prev, xp_ref[...], 0.0).astype(bf16)
    row = lax.broadcasted_iota(jnp.int32, (tm, FFN_COL_TILE), 0)
    for f in range(D_FF // FFN_COL_TILE):
        cs = slice(f * FFN_COL_TILE, (f + 1) * FFN_COL_TILE)
        g = _dot(xb, wi_ref[:, cs])
        u = _dot(xb, wi_ref[:, D_FF + f * FFN_COL_TILE:D_FF + (f + 1) * FFN_COL_TILE])
        gp = _dot(xpb, wi_ref[:, cs])
        g1 = jnp.where(row < 1, gp[HALO_ROWS - 1:HALO_ROWS, :], pltpu.roll(g, 1, 0))
        g2 = jnp.where(row < 2, jnp.where(row < 1, gp[HALO_ROWS - 2:HALO_ROWS - 1, :],
                                          gp[HALO_ROWS - 1:HALO_ROWS, :]), pltpu.roll(g, 2, 0))
        y = cw_ref[0:1, cs] * g2 + cw_ref[1:2, cs] * g1 + cw_ref[2:3, cs] * g + cb_ref[:, cs]
        cdf = 0.5 * (1.0 + jnp.tanh(np.sqrt(2.0 / np.pi).astype(np.float32) * (y + 0.044715 * (y * y * y))))
        h_ref[:, cs] = (y * cdf * u).astype(bf16)
    o_ref[...] = _deepnorm_ln(x, _dot(h_ref[...], wo_ref[...]), g_ref[...], b_ref[...])


def _ffn(x, wi, cw, cb, wo, g, b, S, tm):
    T = x.shape[0]
    tiles_per_seq = S // tm
    halo_blocks = tm // HALO_ROWS
    single = pl.Buffered(1)
    return pl.pallas_call(
        functools.partial(_ffn_kernel, tiles_per_seq=tiles_per_seq),
        grid=(T // tm,),
        in_specs=[
            pl.BlockSpec((tm, D_MODEL), lambda i: (i, 0)),
            pl.BlockSpec((HALO_ROWS, D_MODEL), lambda i: (jnp.maximum(i * halo_blocks - 1, 0), 0)),
            pl.BlockSpec(wi.shape, lambda i: (0, 0), pipeline_mode=single),
            pl.BlockSpec(cw.shape, lambda i: (0, 0)),
            pl.BlockSpec(cb.shape, lambda i: (0, 0)),
            pl.BlockSpec(wo.shape, lambda i: (0, 0), pipeline_mode=single),
            pl.BlockSpec((1, D_MODEL), lambda i: (0, 0)),
            pl.BlockSpec((1, D_MODEL), lambda i: (0, 0)),
        ],
        out_specs=pl.BlockSpec((tm, D_MODEL), lambda i: (i, 0)),
        out_shape=jax.ShapeDtypeStruct((T, D_MODEL), f32),
        scratch_shapes=[pltpu.VMEM((tm, D_FF), bf16)],
        compiler_params=_params(("parallel",)),
        name="conv_ffn",
    )(x, x, wi, cw, cb, wo, g, b)


def _even_weight(w_in):
    qa, ka, va, ra, ga, qb, kb, vb = jnp.split(
        w_in, np.cumsum([256, 256, 512, 512, GLA_GATE_RANK, 512, 512, 512])[:-1].tolist(), axis=1)
    pad = jnp.zeros((D_MODEL, LANES - GLA_GATE_RANK), w_in.dtype)
    return jnp.concatenate(
        [qa * GLA_DK ** -0.5, ka, ra, va, qb * HEAD_DIM ** -0.5, kb, ga, pad, vb], axis=1).astype(bf16)


def _odd_weight(w_in):
    return jnp.concatenate([w_in[:, :1024] * HEAD_DIM ** -0.5, w_in[:, 1024:]], axis=1).astype(bf16)


def kernel(x, mem, positions, even_w_in, even_w_out, gla_gate_w, gla_gate_b, gla_norm_g, diff_lam_q1, diff_lam_k1, diff_lam_q2, diff_lam_k2, diff_norm_g, odd_w_in, odd_w_out, swa_sinks, mem_w_q, mem_w_kv, mem_w_out, ffn_w_in, ffn_conv_w, ffn_conv_b, ffn_w_out, ln_g, ln_b):
    B, S, D = x.shape
    T = B * S
    tm = min(512, S)
    xf = x.reshape(T, D)

    inv_freq = ROPE_THETA ** (-jnp.arange(0, HEAD_DIM, 2, dtype=f32) / HEAD_DIM)
    ang = positions.astype(f32).reshape(T, 1) * inv_freq
    cos, sin = jnp.cos(ang), jnp.sin(ang)
    cos_t = jnp.tile(cos, (1, 4))
    sin_t = jnp.concatenate([-sin, sin, -sin, sin], axis=1)

    memf = mem.reshape(B * mem.shape[1], D)
    row2 = lambda v: v.reshape(1, -1)

    for i in range(DEPTH):
        j = i // 2
        if i % 2 == 0:
            lam_init = 0.8 - 0.6 * float(np.exp(-0.3 * i))
            of, ob, vt = _even_in_proj(xf, _even_weight(even_w_in[j]), cos_t, sin_t, B, S, tm)
            gate_w = jnp.pad(gla_gate_w[j], ((0, LANES - GLA_GATE_RANK), (0, 0))).astype(bf16)
            o_a = _gla(of, ob, gate_w, row2(gla_gate_b[j]), row2(gla_norm_g[j]), B, S, min(4, S // CHUNK))
            o_b = _diff_attention(ob, vt, row2(diff_lam_q1[j]), row2(diff_lam_k1[j]), row2(diff_lam_q2[j]),
                                  row2(diff_lam_k2[j]), diff_norm_g[j].reshape(-1, 1), lam_init, B, S, tm)
            xf = _out_proj_ln([o_a, o_b], even_w_out[j].astype(bf16), xf, row2(ln_g[i, 0]), row2(ln_b[i, 0]), tm)
        else:
            qkv = _odd_in_proj(xf, _odd_weight(odd_w_in[j]), cos_t, sin_t, tm)
            sink_rows = jnp.repeat(swa_sinks[j].reshape(SWA_KV_HEADS, SWA_GROUP), CHUNK, axis=1)[..., None]
            o_c = _swa(qkv, sink_rows, B, S, min(8, S // CHUNK))
            xf = _out_proj_ln([o_c], odd_w_out[j].astype(bf16), xf, row2(ln_g[i, 0]), row2(ln_b[i, 0]), tm)

        kv = _matmul(memf, mem_w_kv[i].astype(bf16), memf.shape[0], bf16)
        xf = _mem_attention(xf, (mem_w_q[i] * MEM_DH ** -0.5).astype(bf16), kv, mem_w_out[i].astype(bf16),
                            row2(ln_g[i, 1]), row2(ln_b[i, 1]), B, S, tm)
        xf = _ffn(xf, ffn_w_in[i].astype(bf16), ffn_conv_w[i], row2(ffn_conv_b[i]), ffn_w_out[i].astype(bf16),
                  row2(ln_g[i, 2]), row2(ln_b[i, 2]), S, tm)
    return xf.reshape(B, S, D)
```

```python
import functools

import numpy as np
import jax
import jax.numpy as jnp
from jax import lax
from jax.experimental import pallas as pl
from jax.experimental.pallas import tpu as pltpu

f32 = jnp.float32
bf16 = jnp.bfloat16

D_MODEL = 1024
DEPTH = 4
CHUNK = 64
ROPE_THETA = 10000.0
LN_EPS = 1e-5
RMS_EPS = 1e-6
NEG_INF = -1e30
HEAD_DIM = 64

GLA_HEADS = 4
GLA_DK = 64
GLA_DV = 128
GLA_GATE_RANK = 16
GLA_TAU = 16.0

DIFF_HEADS = 4
DIFF_DV = 128

SWA_Q_HEADS = 16
SWA_KV_HEADS = 2
SWA_GROUP = SWA_Q_HEADS // SWA_KV_HEADS
WINDOW_CHUNKS = 2

MEM_HEADS = 4
MEM_DH = D_MODEL // MEM_HEADS

D_FF = 2816
DEEPNORM_ALPHA = (2 * DEPTH) ** 0.25
LOG2E = 1.4426950408889634

LANES = 128
VMEM_LIMIT = 56 * 1024 * 1024

_NT = (((1,), (1,)), ((), ()))
_TN = (((0,), (0,)), ((), ()))


def _params(semantics):
    return pltpu.CompilerParams(dimension_semantics=semantics, vmem_limit_bytes=VMEM_LIMIT)


def _dot(a, b):
    return jnp.dot(a, b, preferred_element_type=f32)


def _deepnorm_ln(x, h, g, b):
    y = DEEPNORM_ALPHA * x + h
    mu = jnp.mean(y, axis=-1, keepdims=True)
    d = y - mu
    var = jnp.mean(d * d, axis=-1, keepdims=True)
    return d * lax.rsqrt(var + LN_EPS) * g + b


def _rope(y, cos_t, sin_t):
    rows = y.shape[0]
    lane = lax.broadcasted_iota(jnp.int32, (rows, LANES), 1)
    first_half = (lane & 32) == 0
    outs = []
    for j in range(y.shape[1] // LANES):
        t = y[:, j * LANES:(j + 1) * LANES]
        swapped = jnp.where(first_half, pltpu.roll(t, LANES - 32, 1), pltpu.roll(t, 32, 1))
        outs.append(t * cos_t + swapped * sin_t)
    return jnp.concatenate(outs, axis=1)


EVEN_F32_COLS = 1024
EVEN_BF_COLS = 1664
DIFF_V_COLS = DIFF_HEADS * DIFF_DV


def _even_in_kernel(x_ref, w_ref, cos_ref, sin_ref, of_ref, ob_ref, vt_ref):
    xb = x_ref[...].astype(bf16)
    of_ref[...] = _dot(xb, w_ref[:, 0:1024])
    ob_ref[:, 0:512] = _dot(xb, w_ref[:, 1024:1536]).astype(bf16)
    c = cos_ref[...]
    s = sin_ref[...]
    ob_ref[:, 512:1024] = _rope(_dot(xb, w_ref[:, 1536:2048]), c * LOG2E, s * LOG2E).astype(bf16)
    ob_ref[:, 1024:1536] = _rope(_dot(xb, w_ref[:, 2048:2560]), c, s).astype(bf16)
    ob_ref[:, 1536:1664] = _dot(xb, w_ref[:, 2560:2688]).astype(bf16)
    vt_ref[...] = _dot(xb, w_ref[:, 2688:3200]).T.astype(bf16)


def _even_in_proj(x, w, cos_t, sin_t, B, S, tm):
    T = x.shape[0]
    nt = S // tm
    return pl.pallas_call(
        _even_in_kernel,
        grid=(T // tm,),
        in_specs=[
            pl.BlockSpec((tm, D_MODEL), lambda i: (i, 0)),
            pl.BlockSpec(w.shape, lambda i: (0, 0)),
            pl.BlockSpec((tm, LANES), lambda i: (i, 0)),
            pl.BlockSpec((tm, LANES), lambda i: (i, 0)),
        ],
        out_specs=[
            pl.BlockSpec((tm, EVEN_F32_COLS), lambda i: (i, 0)),
            pl.BlockSpec((tm, EVEN_BF_COLS), lambda i: (i, 0)),
            pl.BlockSpec((DIFF_V_COLS, tm), lambda i: (i // nt, i % nt)),
        ],
        out_shape=[
            jax.ShapeDtypeStruct((T, EVEN_F32_COLS), f32),
            jax.ShapeDtypeStruct((T, EVEN_BF_COLS), bf16),
            jax.ShapeDtypeStruct((B * DIFF_V_COLS, S), bf16),
        ],
        compiler_params=_params(("parallel",)),
        name="even_in_proj",
    )(x, w, cos_t, sin_t)


def _gla_kernel(of_ref, va_ref, ga_ref, gw_ref, gb_ref, ng_ref, o_ref, st_ref, *, nchunk):
    tc = nchunk * CHUNK
    hk = GLA_HEADS * GLA_DK

    @pl.when(pl.program_id(1) == 0)
    def _():
        st_ref[...] = jnp.zeros_like(st_ref)

    z = _dot(ga_ref[...], gw_ref[...]) + gb_ref[...]
    log_a = (jnp.minimum(z, 0.0) - jnp.log(1.0 + jnp.exp(-jnp.abs(z)))) * (1.0 / GLA_TAU)

    row = lax.broadcasted_iota(jnp.int32, (tc, tc), 0)
    col = lax.broadcasted_iota(jnp.int32, (tc, tc), 1)
    tri = jnp.where((col <= row) & ((col // CHUNK) == (row // CHUNK)), 1.0, 0.0).astype(bf16)
    hi = log_a.astype(bf16)
    rem = log_a - hi.astype(f32)
    mid = rem.astype(bf16)
    lo = (rem - mid.astype(f32)).astype(bf16)
    b = _dot(tri, hi) + _dot(tri, mid) + _dot(tri, lo)

    b3 = b.reshape(nchunk, CHUNK, hk)
    b_last = b3[:, CHUNK - 1:CHUNK, :]
    k_carry = jnp.exp(b_last - b3).reshape(tc, hk)
    decay = jnp.exp(b_last)

    qa = of_ref[:, 0:hk]
    ka = of_ref[:, hk:2 * hk]
    q_dec = (qa * jnp.exp(b)).astype(bf16)
    k_neg = (ka * jnp.exp(-b)).astype(bf16)
    k_dec = (ka * k_carry).astype(bf16)

    r = lax.broadcasted_iota(jnp.int32, (CHUNK, CHUNK), 0)
    cc = lax.broadcasted_iota(jnp.int32, (CHUNK, CHUNK), 1)
    causal = cc <= r
    gain = ng_ref[...]

    for h in range(GLA_HEADS):
        ks = slice(h * GLA_DK, (h + 1) * GLA_DK)
        vs = slice(h * GLA_DV, (h + 1) * GLA_DV)
        state_t = st_ref[h]
        for c in range(nchunk):
            rs = slice(c * CHUNK, (c + 1) * CHUNK)
            qd = q_dec[rs, ks]
            v = va_ref[rs, vs]
            a = lax.dot_general(qd, k_neg[rs, ks], _NT, preferred_element_type=f32)
            a = jnp.where(causal, a, 0.0).astype(bf16)
            o = _dot(a, v) + lax.dot_general(qd, state_t.astype(bf16), _NT, preferred_element_type=f32)
            kv_t = lax.dot_general(v, k_dec[rs, ks], _TN, preferred_element_type=f32)
            state_t = decay[c, :, ks] * state_t + kv_t
            o = o * lax.rsqrt(jnp.mean(o * o, axis=-1, keepdims=True) + RMS_EPS) * gain
            rr = of_ref[rs, 2 * hk + h * GLA_DV:2 * hk + (h + 1) * GLA_DV]
            o_ref[rs, vs] = (o * (rr * (1.0 / (1.0 + jnp.exp(-rr))))).astype(bf16)
        st_ref[h] = state_t


def _gla(of, ob, gate_w, gate_b, norm_g, B, S, nchunk):
    T = of.shape[0]
    tc = nchunk * CHUNK
    nt = S // tc
    return pl.pallas_call(
        functools.partial(_gla_kernel, nchunk=nchunk),
        grid=(B, nt),
        in_specs=[
            pl.BlockSpec((tc, EVEN_F32_COLS), lambda b, i: (b * nt + i, 0)),
            pl.BlockSpec((tc, 512), lambda b, i: (b * nt + i, 0)),
            pl.BlockSpec((tc, LANES), lambda b, i: (b * nt + i, 1536 // LANES)),
            pl.BlockSpec(gate_w.shape, lambda b, i: (0, 0)),
            pl.BlockSpec(gate_b.shape, lambda b, i: (0, 0)),
            pl.BlockSpec(norm_g.shape, lambda b, i: (0, 0)),
        ],
        out_specs=pl.BlockSpec((tc, GLA_HEADS * GLA_DV), lambda b, i: (b * nt + i, 0)),
        out_shape=jax.ShapeDtypeStruct((T, GLA_HEADS * GLA_DV), bf16),
        scratch_shapes=[pltpu.VMEM((GLA_HEADS, GLA_DV, GLA_DK), f32)],
        compiler_params=_params(("parallel", "arbitrary")),
        name="gla",
    )(of, ob, ob, gate_w, gate_b, norm_g)


def _diff_kernel(q_ref, k_ref, vt_ref, lq1_ref, lk1_ref, lq2_ref, lk2_ref, ng_ref, o_ref, s_ref, *, tq, tk,
                 lam_init):
    i = pl.program_id(2)
    q = q_ref[...]
    q_maps = (q[:, 0:HEAD_DIM], q[:, HEAD_DIM:2 * HEAD_DIM])

    key_chunk = lax.broadcasted_iota(jnp.int32, (tk, tq), 0) // CHUNK
    qry_chunk = lax.broadcasted_iota(jnp.int32, (tk, tq), 1) // CHUNK

    def produce(j, slot):
        start = pl.multiple_of(j * tk, tk)
        k = k_ref[pl.ds(start, tk), :]
        col_max = []
        for t in range(2):
            s = lax.dot_general(k[:, t * HEAD_DIM:(t + 1) * HEAD_DIM], q_maps[t], _NT,
                                preferred_element_type=f32)
            s_ref[slot, t] = s
            col_max.append(jnp.max(s, axis=0, keepdims=True))
        return tuple(col_max)

    def consume(j, slot, col_max, state, masked):
        start = pl.multiple_of(j * tk, tk)
        vt = vt_ref[:, pl.ds(start, tk)]
        if masked:
            visible = key_chunk + (j - 2 * i) * (tk // CHUNK) <= qry_chunk
        out = []
        for t in range(2):
            m, l, acc = state[3 * t:3 * t + 3]
            s = s_ref[slot, t]
            if masked:
                s = jnp.where(visible, s, NEG_INF)
                tile_max = jnp.max(s, axis=0, keepdims=True)
            else:
                tile_max = col_max[t]
            m_new = jnp.maximum(m, tile_max)
            alpha = jnp.exp2(m - m_new)
            p = jnp.exp2(s - m_new)
            l = alpha * l + jnp.sum(p, axis=0, keepdims=True)
            acc = alpha * acc + _dot(vt, p.astype(bf16))
            out += [m_new, l, acc]
        return tuple(out)

    def pair(jj, carry):
        j = 2 * jj
        max1 = produce(j + 1, 1)
        state = consume(j, 0, carry[6:], carry[:6], False)
        max0 = produce(j + 2, 0)
        state = consume(j + 1, 1, max1, state, False)
        return state + max0

    init = (jnp.full((1, tq), NEG_INF, f32), jnp.zeros((1, tq), f32), jnp.zeros((DIFF_DV, tq), f32)) * 2
    carry = lax.fori_loop(0, i, pair, init + produce(0, 0))
    produce(2 * i + 1, 1)
    state = consume(2 * i, 0, None, carry[:6], True)
    m0, l0, acc0, m1, l1, acc1 = consume(2 * i + 1, 1, None, state, True)

    lam = (jnp.exp(jnp.sum(lq1_ref[...] * lk1_ref[...], axis=1, keepdims=True))
           - jnp.exp(jnp.sum(lq2_ref[...] * lk2_ref[...], axis=1, keepdims=True)) + lam_init)
    o = acc0 * (1.0 / l0) - lam * (acc1 * (1.0 / l1))
    o = o * lax.rsqrt(jnp.mean(o * o, axis=0, keepdims=True) + RMS_EPS) * ng_ref[...]
    o_ref[...] = (o * (1.0 - lam_init)).T.astype(bf16)


def _diff_attention(ob, vt, lq1, lk1, lq2, lk2, norm_g, lam_init, B, S, tq):
    T = ob.shape[0]
    nq = S // tq
    tk = tq // 2
    small = lambda a: pl.BlockSpec(a.shape, lambda b, h, i: (0, 0))
    return pl.pallas_call(
        functools.partial(_diff_kernel, tq=tq, tk=tk, lam_init=lam_init),
        grid=(B, DIFF_HEADS, nq),
        in_specs=[
            pl.BlockSpec((tq, LANES), lambda b, h, i: (b * nq + i, 4 + h)),
            pl.BlockSpec((S, LANES), lambda b, h, i: (b, 8 + h)),
            pl.BlockSpec((DIFF_DV, S), lambda b, h, i: (b * DIFF_HEADS + h, 0)),
            small(lq1), small(lk1), small(lq2), small(lk2), small(norm_g),
        ],
        out_specs=pl.BlockSpec((tq, DIFF_DV), lambda b, h, i: (b * nq + i, h)),
        out_shape=jax.ShapeDtypeStruct((T, DIFF_HEADS * DIFF_DV), bf16),
        scratch_shapes=[pltpu.VMEM((2, 2, tk, tq), f32)],
        compiler_params=_params(("parallel", "parallel", "arbitrary")),
        name="diff_attention",
    )(ob, ob, vt, lq1, lk1, lq2, lk2, norm_g)


def _out_ln_kernel(*refs, n_act):
    acts = refs[:n_act]
    w_ref, x_ref, g_ref, b_ref, o_ref = refs[n_act:]
    h = None
    off = 0
    for a_ref in acts:
        k = a_ref.shape[1]
        part = _dot(a_ref[...], w_ref[off:off + k, :])
        h = part if h is None else h + part
        off += k
    o_ref[...] = _deepnorm_ln(x_ref[...], h, g_ref[...], b_ref[...])


def _out_proj_ln(acts, w, x, g, b, tm):
    T = x.shape[0]
    return pl.pallas_call(
        functools.partial(_out_ln_kernel, n_act=len(acts)),
        grid=(T // tm,),
        in_specs=[pl.BlockSpec((tm, a.shape[1]), lambda i: (i, 0)) for a in acts] + [
            pl.BlockSpec(w.shape, lambda i: (0, 0)),
            pl.BlockSpec((tm, D_MODEL), lambda i: (i, 0)),
            pl.BlockSpec((1, D_MODEL), lambda i: (0, 0)),
            pl.BlockSpec((1, D_MODEL), lambda i: (0, 0)),
        ],
        out_specs=pl.BlockSpec((tm, D_MODEL), lambda i: (i, 0)),
        out_shape=jax.ShapeDtypeStruct((T, D_MODEL), f32),
        compiler_params=_params(("parallel",)),
        name="out_proj_ln",
    )(*acts, w, x, g, b)


SWA_Q_COLS = SWA_Q_HEADS * HEAD_DIM
SWA_KV_COLS = SWA_KV_HEADS * HEAD_DIM
SWA_PAIR = 2 * CHUNK
SWA_BAND = (WINDOW_CHUNKS + 2) * CHUNK


def _odd_in_kernel(x_ref, w_ref, cos_ref, sin_ref, qt_ref, k_ref, vt_ref):
    xb = x_ref[...].astype(bf16)
    c = cos_ref[...]
    s = sin_ref[...]
    qt_ref[...] = _rope(_dot(xb, w_ref[:, 0:SWA_Q_COLS]), c, s).T.astype(bf16)
    kv = _dot(xb, w_ref[:, SWA_Q_COLS:SWA_Q_COLS + 2 * SWA_KV_COLS])
    k_ref[...] = _rope(kv[:, 0:SWA_KV_COLS], c, s).astype(bf16)
    vt_ref[...] = kv[:, SWA_KV_COLS:2 * SWA_KV_COLS].T.astype(bf16)


def _odd_in_proj(x, w, cos_t, sin_t, B, S, tm):
    T = x.shape[0]
    nt = S // tm
    return pl.pallas_call(
        _odd_in_kernel,
        grid=(T // tm,),
        in_specs=[
            pl.BlockSpec((tm, D_MODEL), lambda i: (i, 0)),
            pl.BlockSpec(w.shape, lambda i: (0, 0)),
            pl.BlockSpec((tm, LANES), lambda i: (i, 0)),
            pl.BlockSpec((tm, LANES), lambda i: (i, 0)),
        ],
        out_specs=[
            pl.BlockSpec((SWA_Q_COLS, tm), lambda i: (i // nt, i % nt)),
            pl.BlockSpec((tm, SWA_KV_COLS), lambda i: (i, 0)),
            pl.BlockSpec((SWA_KV_COLS, tm), lambda i: (i // nt, i % nt)),
        ],
        out_shape=[
            jax.ShapeDtypeStruct((B * SWA_Q_COLS, S), bf16),
            jax.ShapeDtypeStruct((T, SWA_KV_COLS), bf16),
            jax.ShapeDtypeStruct((B * SWA_KV_COLS, S), bf16),
        ],
        compiler_params=_params(("parallel",)),
        name="odd_in_proj",
    )(x, w, cos_t, sin_t)


def _swa_kernel(qt_ref, k_ref, vt_ref, sink_ref, o_ref, *, npair):
    i = pl.program_id(1)
    ncol = SWA_GROUP * SWA_PAIR
    key_chunk = lax.broadcasted_iota(jnp.int32, (SWA_BAND, ncol), 0) // CHUNK
    qry_chunk = (lax.broadcasted_iota(jnp.int32, (SWA_BAND, ncol), 1) // CHUNK) % 2
    bias_inner = jnp.where((key_chunk >= qry_chunk) & (key_chunk <= qry_chunk + WINDOW_CHUNKS), 0.0, NEG_INF)
    bias_first = jnp.where(key_chunk <= qry_chunk, 0.0, NEG_INF)
    for c in range(npair):
        pair = i * npair + c
        start = pl.multiple_of(jnp.maximum(pair - 1, 0) * SWA_PAIR, SWA_PAIR)
        kb = k_ref[pl.ds(start, SWA_BAND), :]
        vtb = vt_ref[:, pl.ds(start, SWA_BAND)]
        bias = jnp.where(i == 0, bias_first, bias_inner) if c == 0 else bias_inner
        rs = slice(c * SWA_PAIR, (c + 1) * SWA_PAIR)
        for j in range(SWA_KV_HEADS):
            hs = slice(j * HEAD_DIM, (j + 1) * HEAD_DIM)
            qt = jnp.concatenate(
                [qt_ref[(j * SWA_GROUP + g) * HEAD_DIM:(j * SWA_GROUP + g + 1) * HEAD_DIM, rs]
                 for g in range(SWA_GROUP)], axis=1)
            s = _dot(kb[:, hs], qt) + bias
            sink = sink_ref[j:j + 1, :]
            m = jnp.maximum(jnp.max(s, axis=0, keepdims=True), sink)
            p = jnp.exp(s - m)
            den = jnp.sum(p, axis=0, keepdims=True) + jnp.exp(sink - m)
            o = _dot(vtb[hs, :], p.astype(bf16)) * (1.0 / den)
            for gp in range(SWA_GROUP // 2):
                two_heads = jnp.concatenate([o[:, (2 * gp) * SWA_PAIR:(2 * gp + 1) * SWA_PAIR],
                                             o[:, (2 * gp + 1) * SWA_PAIR:(2 * gp + 2) * SWA_PAIR]], axis=0)
                col = (j * SWA_GROUP + 2 * gp) * HEAD_DIM
                o_ref[rs, col:col + 2 * HEAD_DIM] = two_heads.T.astype(bf16)


def _swa(qt, k, vt, sink_rows, B, S, npair):
    T = k.shape[0]
    tq = npair * SWA_PAIR
    nt = S // tq
    return pl.pallas_call(
        functools.partial(_swa_kernel, npair=npair),
        grid=(B, nt),
        in_specs=[
            pl.BlockSpec((SWA_Q_COLS, tq), lambda b, i: (b, i)),
            pl.BlockSpec((S, SWA_KV_COLS), lambda b, i: (b, 0)),
            pl.BlockSpec((SWA_KV_COLS, S), lambda b, i: (b, 0)),
            pl.BlockSpec(sink_rows.shape, lambda b, i: (0, 0)),
        ],
        out_specs=pl.BlockSpec((tq, SWA_Q_COLS), lambda b, i: (b * nt + i, 0)),
        out_shape=jax.ShapeDtypeStruct((T, SWA_Q_COLS), bf16),
        compiler_params=_params(("parallel", "arbitrary")),
        name="swa",
    )(qt, k, vt, sink_rows)


def _matmul_kernel(a_ref, w_ref, o_ref):
    o_ref[...] = _dot(a_ref[...].astype(bf16), w_ref[...]).astype(o_ref.dtype)


def _matmul(a, w, tm, out_dtype):
    M, K = a.shape
    N = w.shape[1]
    return pl.pallas_call(
        _matmul_kernel,
        grid=(M // tm,),
        in_specs=[pl.BlockSpec((tm, K), lambda i: (i, 0)), pl.BlockSpec((K, N), lambda i: (0, 0))],
        out_specs=pl.BlockSpec((tm, N), lambda i: (i, 0)),
        out_shape=jax.ShapeDtypeStruct((M, N), out_dtype),
        compiler_params=_params(("parallel",)),
        name="matmul",
    )(a, w)


def _mem_attn_kernel(x_ref, wq_ref, kv_ref, wo_ref, g_ref, b_ref, o_ref):
    x = x_ref[...]
    q = _dot(x.astype(bf16), wq_ref[...]).astype(bf16)
    heads = []
    for h in range(MEM_HEADS):
        hs = slice(h * MEM_DH, (h + 1) * MEM_DH)
        s = lax.dot_general(q[:, hs], kv_ref[:, hs], _NT, preferred_element_type=f32)
        p = jnp.exp(s - jnp.max(s, axis=1, keepdims=True))
        inv = 1.0 / jnp.sum(p, axis=1, keepdims=True)
        o = _dot(p.astype(bf16), kv_ref[:, D_MODEL + h * MEM_DH:D_MODEL + (h + 1) * MEM_DH]) * inv
        heads.append(o.astype(bf16))
    o = jnp.concatenate(heads, axis=1)
    o_ref[...] = _deepnorm_ln(x, _dot(o, wo_ref[...]), g_ref[...], b_ref[...])


def _mem_attention(x, wq, kv, wo, g, b, B, S, tm):
    T = x.shape[0]
    nt = S // tm
    M = kv.shape[0] // B
    return pl.pallas_call(
        _mem_attn_kernel,
        grid=(B, nt),
        in_specs=[
            pl.BlockSpec((tm, D_MODEL), lambda bb, i: (bb * nt + i, 0)),
            pl.BlockSpec(wq.shape, lambda bb, i: (0, 0)),
            pl.BlockSpec((M, 2 * D_MODEL), lambda bb, i: (bb, 0)),
            pl.BlockSpec(wo.shape, lambda bb, i: (0, 0)),
            pl.BlockSpec((1, D_MODEL), lambda bb, i: (0, 0)),
            pl.BlockSpec((1, D_MODEL), lambda bb, i: (0, 0)),
        ],
        out_specs=pl.BlockSpec((tm, D_MODEL), lambda bb, i: (bb * nt + i, 0)),
        out_shape=jax.ShapeDtypeStruct((T, D_MODEL), f32),
        compiler_params=_params(("parallel", "parallel")),
        name="mem_attention",
    )(x, wq, kv, wo, g, b)


FFN_COL_TILE = 256
HALO_ROWS = 8


def _ffn_kernel(x_ref, xp_ref, wi_ref, cw_ref, cb_ref, wo_ref, g_ref, b_ref, o_ref, h_ref, *, tiles_per_seq):
    tm = x_ref.shape[0]
    x = x_ref[...]
    xb = x.astype(bf16)
    has_prev = (pl.program_id(0) % tiles_per_seq) != 0
    xpb = jnp.where(has_prev, xp_ref[...], 0.0).astype(bf16)
    row = lax.broadcasted_iota(jnp.int32, (tm, FFN_COL_TILE), 0)
    for f in range(D_FF // FFN_COL_TILE):
        cs = slice(f * FFN_COL_TILE, (f + 1) * FFN_COL_TILE)
        g = _dot(xb, wi_ref[:, cs])
        u = _dot(xb, wi_ref[:, D_FF + f * FFN_COL_TILE:D_FF + (f + 1) * FFN_COL_TILE])
        gp = _dot(xpb, wi_ref[:, cs])
        g1 = jnp.where(row < 1, gp[HALO_ROWS - 1:HALO_ROWS, :], pltpu.roll(g, 1, 0))
        g2 = jnp.where(row < 2, jnp.where(row < 1, gp[HALO_ROWS - 2:HALO_ROWS - 1, :],
                                          gp[HALO_ROWS - 1:HALO_ROWS, :]), pltpu.roll(g, 2, 0))
        y = cw_ref[0:1, cs] * g2 + cw_ref[1:2, cs] * g1 + cw_ref[2:3, cs] * g + cb_ref[:, cs]
        cdf = 0.5 * (1.0 + jnp.tanh(np.sqrt(2.0 / np.pi).astype(np.float32) * (y + 0.044715 * (y * y * y))))
        h_ref[:, cs] = (y * cdf * u).astype(bf16)
    o_ref[...] = _deepnorm_ln(x, _dot(h_ref[...], wo_ref[...]), g_ref[...], b_ref[...])


def _ffn(x, wi, cw, cb, wo, g, b, S, tm):
    T = x.shape[0]
    tiles_per_seq = S // tm
    halo_blocks = tm // HALO_ROWS
    single = pl.Buffered(1)
    return pl.pallas_call(
        functools.partial(_ffn_kernel, tiles_per_seq=tiles_per_seq),
        grid=(T // tm,),
        in_specs=[
            pl.BlockSpec((tm, D_MODEL), lambda i: (i, 0)),
            pl.BlockSpec((HALO_ROWS, D_MODEL), lambda i: (jnp.maximum(i * halo_blocks - 1, 0), 0)),
            pl.BlockSpec(wi.shape, lambda i: (0, 0), pipeline_mode=single),
            pl.BlockSpec(cw.shape, lambda i: (0, 0)),
            pl.BlockSpec(cb.shape, lambda i: (0, 0)),
            pl.BlockSpec(wo.shape, lambda i: (0, 0), pipeline_mode=single),
            pl.BlockSpec((1, D_MODEL), lambda i: (0, 0)),
            pl.BlockSpec((1, D_MODEL), lambda i: (0, 0)),
        ],
        out_specs=pl.BlockSpec((tm, D_MODEL), lambda i: (i, 0)),
        out_shape=jax.ShapeDtypeStruct((T, D_MODEL), f32),
        scratch_shapes=[pltpu.VMEM((tm, D_FF), bf16)],
        compiler_params=_params(("parallel",)),
        name="conv_ffn",
    )(x, x, wi, cw, cb, wo, g, b)


def _even_weight(w_in):
    qa, ka, va, ra, ga, qb, kb, vb = jnp.split(
        w_in, np.cumsum([256, 256, 512, 512, GLA_GATE_RANK, 512, 512, 512])[:-1].tolist(), axis=1)
    pad = jnp.zeros((D_MODEL, LANES - GLA_GATE_RANK), w_in.dtype)
    return jnp.concatenate(
        [qa * GLA_DK ** -0.5, ka, ra, va, qb * HEAD_DIM ** -0.5, kb, ga, pad, vb], axis=1).astype(bf16)


def _odd_weight(w_in):
    return jnp.concatenate([w_in[:, :1024] * HEAD_DIM ** -0.5, w_in[:, 1024:]], axis=1).astype(bf16)


def kernel(x, mem, positions, even_w_in, even_w_out, gla_gate_w, gla_gate_b, gla_norm_g, diff_lam_q1, diff_lam_k1, diff_lam_q2, diff_lam_k2, diff_norm_g, odd_w_in, odd_w_out, swa_sinks, mem_w_q, mem_w_kv, mem_w_out, ffn_w_in, ffn_conv_w, ffn_conv_b, ffn_w_out, ln_g, ln_b):
    B, S, D = x.shape
    T = B * S
    tm = min(512, S)
    xf = x.reshape(T, D)

    inv_freq = ROPE_THETA ** (-jnp.arange(0, HEAD_DIM, 2, dtype=f32) / HEAD_DIM)
    ang = positions.astype(f32).reshape(T, 1) * inv_freq
    cos, sin = jnp.cos(ang), jnp.sin(ang)
    cos_t = jnp.tile(cos, (1, 4))
    sin_t = jnp.concatenate([-sin, sin, -sin, sin], axis=1)

    memf = mem.reshape(B * mem.shape[1], D)
    row2 = lambda v: v.reshape(1, -1)

    for i in range(DEPTH):
        j = i // 2
        if i % 2 == 0:
            lam_init = 0.8 - 0.6 * float(np.exp(-0.3 * i))
            of, ob, vt = _even_in_proj(xf, _even_weight(even_w_in[j]), cos_t, sin_t, B, S, tm)
            gate_w = jnp.pad(gla_gate_w[j], ((0, LANES - GLA_GATE_RANK), (0, 0))).astype(bf16)
            o_a = _gla(of, ob, gate_w, row2(gla_gate_b[j]), row2(gla_norm_g[j]), B, S, min(4, S // CHUNK))
            o_b = _diff_attention(ob, vt, row2(diff_lam_q1[j]), row2(diff_lam_k1[j]), row2(diff_lam_q2[j]),
                                  row2(diff_lam_k2[j]), diff_norm_g[j].reshape(-1, 1), lam_init, B, S,
                                  min(1024, S))
            xf = _out_proj_ln([o_a, o_b], even_w_out[j].astype(bf16), xf, row2(ln_g[i, 0]), row2(ln_b[i, 0]), tm)
        else:
            qt, k, vt = _odd_in_proj(xf, _odd_weight(odd_w_in[j]), cos_t, sin_t, B, S, tm)
            sink_rows = jnp.repeat(swa_sinks[j].reshape(SWA_KV_HEADS, SWA_GROUP), SWA_PAIR, axis=1)
            o_c = _swa(qt, k, vt, sink_rows, B, S, min(4, S // SWA_PAIR))
            xf = _out_proj_ln([o_c], odd_w_out[j].astype(bf16), xf, row2(ln_g[i, 0]), row2(ln_b[i, 0]), tm)

        kv = _matmul(memf, mem_w_kv[i].astype(bf16), memf.shape[0], bf16)
        xf = _mem_attention(xf, (mem_w_q[i] * MEM_DH ** -0.5).astype(bf16), kv, mem_w_out[i].astype(bf16),
                            row2(ln_g[i, 1]), row2(ln_b[i, 1]), B, S, tm)
        xf = _ffn(xf, ffn_w_in[i].astype(bf16), ffn_conv_w[i], row2(ffn_conv_b[i]), ffn_w_out[i].astype(bf16),
                  row2(ln_g[i, 2]), row2(ln_b[i, 2]), S, tm)
    return xf.reshape(B, S, D)
```

```python
import functools

import numpy as np
import jax
import jax.numpy as jnp
from jax import lax
from jax.experimental import pallas as pl
from jax.experimental.pallas import tpu as pltpu

f32 = jnp.float32
bf16 = jnp.bfloat16

D_MODEL = 1024
DEPTH = 4
CHUNK = 64
ROPE_THETA = 10000.0
LN_EPS = 1e-5
RMS_EPS = 1e-6
NEG_INF = -1e30
HEAD_DIM = 64

GLA_HEADS = 4
GLA_DK = 64
GLA_DV = 128
GLA_GATE_RANK = 16
GLA_TAU = 16.0

DIFF_HEADS = 4
DIFF_DV = 128

SWA_Q_HEADS = 16
SWA_KV_HEADS = 2
SWA_GROUP = SWA_Q_HEADS // SWA_KV_HEADS
WINDOW_CHUNKS = 2

MEM_HEADS = 4
MEM_DH = D_MODEL // MEM_HEADS

D_FF = 2816
DEEPNORM_ALPHA = (2 * DEPTH) ** 0.25
LOG2E = 1.4426950408889634

LANES = 128
VMEM_LIMIT = 56 * 1024 * 1024

_NT = (((1,), (1,)), ((), ()))
_TN = (((0,), (0,)), ((), ()))


def _params(semantics):
    return pltpu.CompilerParams(dimension_semantics=semantics, vmem_limit_bytes=VMEM_LIMIT)


def _dot(a, b):
    return jnp.dot(a, b, preferred_element_type=f32)


def _deepnorm_ln(x, h, g, b):
    y = DEEPNORM_ALPHA * x + h
    mu = jnp.mean(y, axis=-1, keepdims=True)
    d = y - mu
    var = jnp.mean(d * d, axis=-1, keepdims=True)
    return d * lax.rsqrt(var + LN_EPS) * g + b


def _rope(y, cos_t, sin_t):
    rows = y.shape[0]
    lane = lax.broadcasted_iota(jnp.int32, (rows, LANES), 1)
    first_half = (lane & 32) == 0
    outs = []
    for j in range(y.shape[1] // LANES):
        t = y[:, j * LANES:(j + 1) * LANES]
        swapped = jnp.where(first_half, pltpu.roll(t, LANES - 32, 1), pltpu.roll(t, 32, 1))
        outs.append(t * cos_t + swapped * sin_t)
    return jnp.concatenate(outs, axis=1)


EVEN_F32_COLS = 1024
EVEN_BF_COLS = 1664
DIFF_V_COLS = DIFF_HEADS * DIFF_DV


def _even_in_kernel(x_ref, w_ref, cos_ref, sin_ref, of_ref, ob_ref, vt_ref):
    xb = x_ref[...].astype(bf16)
    of_ref[...] = _dot(xb, w_ref[:, 0:1024])
    ob_ref[:, 0:512] = _dot(xb, w_ref[:, 1024:1536]).astype(bf16)
    c = cos_ref[...]
    s = sin_ref[...]
    ob_ref[:, 512:1024] = _rope(_dot(xb, w_ref[:, 1536:2048]), c * LOG2E, s * LOG2E).astype(bf16)
    ob_ref[:, 1024:1536] = _rope(_dot(xb, w_ref[:, 2048:2560]), c, s).astype(bf16)
    ob_ref[:, 1536:1664] = _dot(xb, w_ref[:, 2560:2688]).astype(bf16)
    vt_ref[...] = _dot(xb, w_ref[:, 2688:3200]).T.astype(bf16)


def _even_in_proj(x, w, cos_t, sin_t, B, S, tm):
    T = x.shape[0]
    nt = S // tm
    return pl.pallas_call(
        _even_in_kernel,
        grid=(T // tm,),
        in_specs=[
            pl.BlockSpec((tm, D_MODEL), lambda i: (i, 0)),
            pl.BlockSpec(w.shape, lambda i: (0, 0)),
            pl.BlockSpec((tm, LANES), lambda i: (i, 0)),
            pl.BlockSpec((tm, LANES), lambda i: (i, 0)),
        ],
        out_specs=[
            pl.BlockSpec((tm, EVEN_F32_COLS), lambda i: (i, 0)),
            pl.BlockSpec((tm, EVEN_BF_COLS), lambda i: (i, 0)),
            pl.BlockSpec((DIFF_V_COLS, tm), lambda i: (i // nt, i % nt)),
        ],
        out_shape=[
            jax.ShapeDtypeStruct((T, EVEN_F32_COLS), f32),
            jax.ShapeDtypeStruct((T, EVEN_BF_COLS), bf16),
            jax.ShapeDtypeStruct((B * DIFF_V_COLS, S), bf16),
        ],
        compiler_params=_params(("parallel",)),
        name="even_in_proj",
    )(x, w, cos_t, sin_t)


def _gla_kernel(of_ref, va_ref, ga_ref, gw_ref, gb_ref, ng_ref, o_ref, st_ref, *, nchunk):
    nb = of_ref.shape[0]
    tc = nchunk * CHUNK
    hk = GLA_HEADS * GLA_DK

    @pl.when(pl.program_id(0) == 0)
    def _():
        st_ref[...] = jnp.zeros_like(st_ref)

    row = lax.broadcasted_iota(jnp.int32, (tc, tc), 0)
    col = lax.broadcasted_iota(jnp.int32, (tc, tc), 1)
    causal = (col <= row) & ((col // CHUNK) == (row // CHUNK))
    tri = jnp.where(causal, 1.0, 0.0).astype(bf16)
    own_chunk = (lax.broadcasted_iota(jnp.int32, (tc, nchunk * GLA_DK), 0) // CHUNK
                 == lax.broadcasted_iota(jnp.int32, (tc, nchunk * GLA_DK), 1) // GLA_DK)
    gain = ng_ref[...]

    for bi in range(nb):
        z = _dot(ga_ref[bi], gw_ref[...]) + gb_ref[...]
        log_a = (jnp.minimum(z, 0.0) - jnp.log(1.0 + jnp.exp(-jnp.abs(z)))) * (1.0 / GLA_TAU)

        hi = log_a.astype(bf16)
        rem = log_a - hi.astype(f32)
        mid = rem.astype(bf16)
        lo = (rem - mid.astype(f32)).astype(bf16)
        b = _dot(tri, hi) + _dot(tri, mid) + _dot(tri, lo)

        b3 = b.reshape(nchunk, CHUNK, hk)
        b_last = b3[:, CHUNK - 1:CHUNK, :]
        k_carry = jnp.exp(b_last - b3).reshape(tc, hk)
        decay = jnp.exp(b_last)

        qa = of_ref[bi, :, 0:hk]
        ka = of_ref[bi, :, hk:2 * hk]
        q_dec = (qa * jnp.exp(b)).astype(bf16)
        k_neg = (ka * jnp.exp(-b)).astype(bf16)
        k_dec = (ka * k_carry).astype(bf16)

        for h in range(GLA_HEADS):
            ks = slice(h * GLA_DK, (h + 1) * GLA_DK)
            vs = slice(h * GLA_DV, (h + 1) * GLA_DV)
            qd = q_dec[:, ks]
            v = va_ref[bi, :, vs]
            a = lax.dot_general(qd, k_neg[:, ks], _NT, preferred_element_type=f32)
            o = _dot(jnp.where(causal, a, 0.0).astype(bf16), v)
            k_blocks = jnp.where(own_chunk, jnp.concatenate([k_dec[:, ks]] * nchunk, axis=1), 0.0)
            kv_t = lax.dot_general(v, k_blocks, _TN, preferred_element_type=f32)
            state_t = st_ref[bi, h]
            states = []
            for c in range(nchunk):
                states.append(state_t.astype(bf16))
                state_t = decay[c, :, ks] * state_t + kv_t[:, c * GLA_DK:(c + 1) * GLA_DK]
            st_ref[bi, h] = state_t
            read = lax.dot_general(qd, jnp.concatenate(states, axis=0), _NT,
                                   preferred_element_type=f32)
            o = o + jnp.concatenate([read[c * CHUNK:(c + 1) * CHUNK, c * GLA_DV:(c + 1) * GLA_DV]
                                     for c in range(nchunk)], axis=0)
            o = o * lax.rsqrt(jnp.mean(o * o, axis=-1, keepdims=True) + RMS_EPS) * gain
            rr = of_ref[bi, :, 2 * hk + h * GLA_DV:2 * hk + (h + 1) * GLA_DV]
            o_ref[bi, :, vs] = (o * (rr * (1.0 / (1.0 + jnp.exp(-rr))))).astype(bf16)


def _gla(of, ob, gate_w, gate_b, norm_g, B, S, nchunk):
    T = of.shape[0]
    tc = nchunk * CHUNK
    full = lambda a: pl.BlockSpec(a.shape, lambda i: (0, 0))
    out = pl.pallas_call(
        functools.partial(_gla_kernel, nchunk=nchunk),
        grid=(S // tc,),
        in_specs=[
            pl.BlockSpec((B, tc, EVEN_F32_COLS), lambda i: (0, i, 0)),
            pl.BlockSpec((B, tc, 512), lambda i: (0, i, 0)),
            pl.BlockSpec((B, tc, LANES), lambda i: (0, i, 1536 // LANES)),
            full(gate_w), full(gate_b), full(norm_g),
        ],
        out_specs=pl.BlockSpec((B, tc, GLA_HEADS * GLA_DV), lambda i: (0, i, 0)),
        out_shape=jax.ShapeDtypeStruct((B, S, GLA_HEADS * GLA_DV), bf16),
        scratch_shapes=[pltpu.VMEM((B, GLA_HEADS, GLA_DV, GLA_DK), f32)],
        compiler_params=_params(("arbitrary",)),
        name="gla",
    )(of.reshape(B, S, -1), ob.reshape(B, S, -1), ob.reshape(B, S, -1), gate_w, gate_b, norm_g)
    return out.reshape(T, -1)


def _diff_kernel(q_ref, k_ref, vt_ref, lq1_ref, lk1_ref, lq2_ref, lk2_ref, ng_ref, o_ref, s_ref, *, tq, tk,
                 lam_init):
    i = pl.program_id(2)
    q = q_ref[...]
    q_maps = (q[:, 0:HEAD_DIM], q[:, HEAD_DIM:2 * HEAD_DIM])

    key_chunk = lax.broadcasted_iota(jnp.int32, (tk, tq), 0) // CHUNK
    qry_chunk = lax.broadcasted_iota(jnp.int32, (tk, tq), 1) // CHUNK

    def produce(j, slot):
        start = pl.multiple_of(j * tk, tk)
        k = k_ref[pl.ds(start, tk), :]
        col_max = []
        for t in range(2):
            s = lax.dot_general(k[:, t * HEAD_DIM:(t + 1) * HEAD_DIM], q_maps[t], _NT,
                                preferred_element_type=f32)
            s_ref[slot, t] = s
            col_max.append(jnp.max(s, axis=0, keepdims=True))
        return tuple(col_max)

    def consume(j, slot, col_max, state, masked):
        start = pl.multiple_of(j * tk, tk)
        vt = vt_ref[:, pl.ds(start, tk)]
        if masked:
            visible = key_chunk + (j - 2 * i) * (tk // CHUNK) <= qry_chunk
        out = []
        for t in range(2):
            m, l, acc = state[3 * t:3 * t + 3]
            s = s_ref[slot, t]
            if masked:
                s = jnp.where(visible, s, NEG_INF)
                tile_max = jnp.max(s, axis=0, keepdims=True)
            else:
                tile_max = col_max[t]
            m_new = jnp.maximum(m, tile_max)
            alpha = jnp.exp2(m - m_new)
            p = jnp.exp2(s - m_new)
            l = alpha * l + jnp.sum(p, axis=0, keepdims=True)
            acc = alpha * acc + _dot(vt, p.astype(bf16))
            out += [m_new, l, acc]
        return tuple(out)

    def pair(jj, carry):
        j = 2 * jj
        max1 = produce(j + 1, 1)
        state = consume(j, 0, carry[6:], carry[:6], False)
        max0 = produce(j + 2, 0)
        state = consume(j + 1, 1, max1, state, False)
        return state + max0

    init = (jnp.full((1, tq), NEG_INF, f32), jnp.zeros((1, tq), f32), jnp.zeros((DIFF_DV, tq), f32)) * 2
    carry = lax.fori_loop(0, i, pair, init + produce(0, 0))
    produce(2 * i + 1, 1)
    state = consume(2 * i, 0, None, carry[:6], True)
    m0, l0, acc0, m1, l1, acc1 = consume(2 * i + 1, 1, None, state, True)

    lam = (jnp.exp(jnp.sum(lq1_ref[...] * lk1_ref[...], axis=1, keepdims=True))
           - jnp.exp(jnp.sum(lq2_ref[...] * lk2_ref[...], axis=1, keepdims=True)) + lam_init)
    o = acc0 * (1.0 / l0) - lam * (acc1 * (1.0 / l1))
    o = o * lax.rsqrt(jnp.mean(o * o, axis=0, keepdims=True) + RMS_EPS) * ng_ref[...]
    o_ref[...] = (o * (1.0 - lam_init)).T.astype(bf16)


def _diff_attention(ob, vt, lq1, lk1, lq2, lk2, norm_g, lam_init, B, S, tq):
    T = ob.shape[0]
    nq = S // tq
    tk = tq // 2
    small = lambda a: pl.BlockSpec(a.shape, lambda b, h, i: (0, 0))
    return pl.pallas_call(
        functools.partial(_diff_kernel, tq=tq, tk=tk, lam_init=lam_init),
        grid=(B, DIFF_HEADS, nq),
        in_specs=[
            pl.BlockSpec((tq, LANES), lambda b, h, i: (b * nq + i, 4 + h)),
            pl.BlockSpec((S, LANES), lambda b, h, i: (b, 8 + h)),
            pl.BlockSpec((DIFF_DV, S), lambda b, h, i: (b * DIFF_HEADS + h, 0)),
            small(lq1), small(lk1), small(lq2), small(lk2), small(norm_g),
        ],
        out_specs=pl.BlockSpec((tq, DIFF_DV), lambda b, h, i: (b * nq + i, h)),
        out_shape=jax.ShapeDtypeStruct((T, DIFF_HEADS * DIFF_DV), bf16),
        scratch_shapes=[pltpu.VMEM((2, 2, tk, tq), f32)],
        compiler_params=_params(("parallel", "parallel", "arbitrary")),
        name="diff_attention",
    )(ob, ob, vt, lq1, lk1, lq2, lk2, norm_g)


SWA_Q_COLS = SWA_Q_HEADS * HEAD_DIM
SWA_KV_COLS = SWA_KV_HEADS * HEAD_DIM
SWA_PAIR = 2 * CHUNK
SWA_BAND = (WINDOW_CHUNKS + 2) * CHUNK


def _odd_in_kernel(x_ref, w_ref, cos_ref, sin_ref, qt_ref, k_ref, vt_ref):
    xb = x_ref[...].astype(bf16)
    c = cos_ref[...]
    s = sin_ref[...]
    qt_ref[...] = _rope(_dot(xb, w_ref[:, 0:SWA_Q_COLS]), c, s).T.astype(bf16)
    kv = _dot(xb, w_ref[:, SWA_Q_COLS:SWA_Q_COLS + 2 * SWA_KV_COLS])
    k_ref[...] = _rope(kv[:, 0:SWA_KV_COLS], c, s).astype(bf16)
    vt_ref[...] = kv[:, SWA_KV_COLS:2 * SWA_KV_COLS].T.astype(bf16)


def _odd_in_proj(x, w, cos_t, sin_t, B, S, tm):
    T = x.shape[0]
    nt = S // tm
    return pl.pallas_call(
        _odd_in_kernel,
        grid=(T // tm,),
        in_specs=[
            pl.BlockSpec((tm, D_MODEL), lambda i: (i, 0)),
            pl.BlockSpec(w.shape, lambda i: (0, 0)),
            pl.BlockSpec((tm, LANES), lambda i: (i, 0)),
            pl.BlockSpec((tm, LANES), lambda i: (i, 0)),
        ],
        out_specs=[
            pl.BlockSpec((SWA_Q_COLS, tm), lambda i: (i // nt, i % nt)),
            pl.BlockSpec((tm, SWA_KV_COLS), lambda i: (i, 0)),
            pl.BlockSpec((SWA_KV_COLS, tm), lambda i: (i // nt, i % nt)),
        ],
        out_shape=[
            jax.ShapeDtypeStruct((B * SWA_Q_COLS, S), bf16),
            jax.ShapeDtypeStruct((T, SWA_KV_COLS), bf16),
            jax.ShapeDtypeStruct((B * SWA_KV_COLS, S), bf16),
        ],
        compiler_params=_params(("parallel",)),
        name="odd_in_proj",
    )(x, w, cos_t, sin_t)


def _swa_kernel(qt_ref, k_ref, vt_ref, sink_ref, o_ref, *, npair):
    i = pl.program_id(1)
    ncol = SWA_GROUP * SWA_PAIR
    key_chunk = lax.broadcasted_iota(jnp.int32, (SWA_BAND, ncol), 0) // CHUNK
    qry_chunk = (lax.broadcasted_iota(jnp.int32, (SWA_BAND, ncol), 1) // CHUNK) % 2
    bias_inner = jnp.where((key_chunk >= qry_chunk) & (key_chunk <= qry_chunk + WINDOW_CHUNKS), 0.0, NEG_INF)
    bias_first = jnp.where(key_chunk <= qry_chunk, 0.0, NEG_INF)
    for c in range(npair):
        pair = i * npair + c
        start = pl.multiple_of(jnp.maximum(pair - 1, 0) * SWA_PAIR, SWA_PAIR)
        kb = k_ref[pl.ds(start, SWA_BAND), :]
        vtb = vt_ref[:, pl.ds(start, SWA_BAND)]
        bias = jnp.where(i == 0, bias_first, bias_inner) if c == 0 else bias_inner
        rs = slice(c * SWA_PAIR, (c + 1) * SWA_PAIR)
        for j in range(SWA_KV_HEADS):
            hs = slice(j * HEAD_DIM, (j + 1) * HEAD_DIM)
            qt = jnp.concatenate(
                [qt_ref[(j * SWA_GROUP + g) * HEAD_DIM:(j * SWA_GROUP + g + 1) * HEAD_DIM, rs]
                 for g in range(SWA_GROUP)], axis=1)
            s = _dot(kb[:, hs], qt) + bias
            sink = sink_ref[j:j + 1, :]
            m = jnp.maximum(jnp.max(s, axis=0, keepdims=True), sink)
            p = jnp.exp(s - m)
            den = jnp.sum(p, axis=0, keepdims=True) + jnp.exp(sink - m)
            o = _dot(vtb[hs, :], p.astype(bf16)) * (1.0 / den)
            for gp in range(SWA_GROUP // 2):
                two_heads = jnp.concatenate([o[:, (2 * gp) * SWA_PAIR:(2 * gp + 1) * SWA_PAIR],
                                             o[:, (2 * gp + 1) * SWA_PAIR:(2 * gp + 2) * SWA_PAIR]], axis=0)
                col = (j * SWA_GROUP + 2 * gp) * HEAD_DIM
                o_ref[rs, col:col + 2 * HEAD_DIM] = two_heads.T.astype(bf16)


def _swa(qt, k, vt, sink_rows, B, S, npair):
    T = k.shape[0]
    tq = npair * SWA_PAIR
    nt = S // tq
    return pl.pallas_call(
        functools.partial(_swa_kernel, npair=npair),
        grid=(B, nt),
        in_specs=[
            pl.BlockSpec((SWA_Q_COLS, tq), lambda b, i: (b, i)),
            pl.BlockSpec((S, SWA_KV_COLS), lambda b, i: (b, 0)),
            pl.BlockSpec((SWA_KV_COLS, S), lambda b, i: (b, 0)),
            pl.BlockSpec(sink_rows.shape, lambda b, i: (0, 0)),
        ],
        out_specs=pl.BlockSpec((tq, SWA_Q_COLS), lambda b, i: (b * nt + i, 0)),
        out_shape=jax.ShapeDtypeStruct((T, SWA_Q_COLS), bf16),
        compiler_params=_params(("parallel", "arbitrary")),
        name="swa",
    )(qt, k, vt, sink_rows)


def _matmul_kernel(a_ref, w_ref, o_ref):
    o_ref[...] = _dot(a_ref[...].astype(bf16), w_ref[...]).astype(o_ref.dtype)


def _matmul(a, w, tm, out_dtype):
    M, K = a.shape
    N = w.shape[1]
    return pl.pallas_call(
        _matmul_kernel,
        grid=(M // tm,),
        in_specs=[pl.BlockSpec((tm, K), lambda i: (i, 0)), pl.BlockSpec((K, N), lambda i: (0, 0))],
        out_specs=pl.BlockSpec((tm, N), lambda i: (i, 0)),
        out_shape=jax.ShapeDtypeStruct((M, N), out_dtype),
        compiler_params=_params(("parallel",)),
        name="matmul",
    )(a, w)


FFN_COL_TILE = 256
HALO_ROWS = 8


def _mem_attention_block(x, wq_ref, kv_ref, wo_ref):
    q = _dot(x.astype(bf16), wq_ref[...]).astype(bf16)
    heads = []
    for h in range(MEM_HEADS):
        hs = slice(h * MEM_DH, (h + 1) * MEM_DH)
        s = lax.dot_general(q[:, hs], kv_ref[:, hs], _NT, preferred_element_type=f32)
        p = jnp.exp(s - jnp.max(s, axis=1, keepdims=True))
        inv = 1.0 / jnp.sum(p, axis=1, keepdims=True)
        o = _dot(p.astype(bf16), kv_ref[:, D_MODEL + h * MEM_DH:D_MODEL + (h + 1) * MEM_DH]) * inv
        heads.append(o.astype(bf16))
    return _dot(jnp.concatenate(heads, axis=1), wo_ref[...])


def _conv_ffn_block(x, x_prev, wi_ref, cw_ref, cb_ref, wo_ref, h_ref):
    tm = x.shape[0]
    xb = x.astype(bf16)
    xpb = x_prev.astype(bf16)
    row = lax.broadcasted_iota(jnp.int32, (tm, FFN_COL_TILE), 0)
    for f in range(D_FF // FFN_COL_TILE):
        cs = slice(f * FFN_COL_TILE, (f + 1) * FFN_COL_TILE)
        g = _dot(xb, wi_ref[:, cs])
        u = _dot(xb, wi_ref[:, D_FF + f * FFN_COL_TILE:D_FF + (f + 1) * FFN_COL_TILE])
        gp = _dot(xpb, wi_ref[:, cs])
        g1 = jnp.where(row < 1, gp[HALO_ROWS - 1:HALO_ROWS, :], pltpu.roll(g, 1, 0))
        g2 = jnp.where(row < 2, jnp.where(row < 1, gp[HALO_ROWS - 2:HALO_ROWS - 1, :],
                                          gp[HALO_ROWS - 1:HALO_ROWS, :]), pltpu.roll(g, 2, 0))
        y = cw_ref[0:1, cs] * g2 + cw_ref[1:2, cs] * g1 + cw_ref[2:3, cs] * g + cb_ref[:, cs]
        cdf = 0.5 * (1.0 + jnp.tanh(np.sqrt(2.0 / np.pi).astype(np.float32) * (y + 0.044715 * (y * y * y))))
        h_ref[:, cs] = (y * cdf * u).astype(bf16)
    return _dot(h_ref[...], wo_ref[...])


def _post_kernel(*refs, n_act, tiles_per_seq):
    acts = refs[:n_act]
    (w_mix_ref, x_ref, lng_ref, lnb_ref, wq_ref, kv_ref, wo_ref, wi_ref, cw_ref, cb_ref, wf_ref,
     o_ref, h_ref, halo_ref) = refs[n_act:]
    tm = x_ref.shape[0]
    i = pl.program_id(0)

    @pl.when(i == 0)
    def _():
        halo_ref[...] = jnp.zeros_like(halo_ref)

    h = None
    off = 0
    for a_ref in acts:
        k = a_ref.shape[1]
        part = _dot(a_ref[...], w_mix_ref[off:off + k, :])
        h = part if h is None else h + part
        off += k
    x = _deepnorm_ln(x_ref[...], h, lng_ref[0:1, :], lnb_ref[0:1, :])
    x = _deepnorm_ln(x, _mem_attention_block(x, wq_ref, kv_ref, wo_ref), lng_ref[1:2, :], lnb_ref[1:2, :])
    x_prev = jnp.where((i % tiles_per_seq) != 0, halo_ref[...], 0.0)
    halo_ref[...] = x[tm - HALO_ROWS:, :]
    h = _conv_ffn_block(x, x_prev, wi_ref, cw_ref, cb_ref, wf_ref, h_ref)
    o_ref[...] = _deepnorm_ln(x, h, lng_ref[2:3, :], lnb_ref[2:3, :])


def _post_mixer(acts, w_mix, x, ln_g, ln_b, wq, kv, wo, wi, cw, cb, wf, B, S, tm):
    T = x.shape[0]
    tiles_per_seq = S // tm
    M = kv.shape[0] // B
    resident = lambda a: pl.BlockSpec(a.shape, lambda i: (0, 0), pipeline_mode=pl.Buffered(1))
    small = lambda a: pl.BlockSpec(a.shape, lambda i: (0, 0))
    return pl.pallas_call(
        functools.partial(_post_kernel, n_act=len(acts), tiles_per_seq=tiles_per_seq),
        grid=(T // tm,),
        in_specs=[pl.BlockSpec((tm, a.shape[1]), lambda i: (i, 0)) for a in acts] + [
            resident(w_mix),
            pl.BlockSpec((tm, D_MODEL), lambda i: (i, 0)),
            small(ln_g), small(ln_b),
            resident(wq),
            pl.BlockSpec((M, 2 * D_MODEL), lambda i: (i // tiles_per_seq, 0)),
            resident(wo), resident(wi), small(cw), small(cb), resident(wf),
        ],
        out_specs=pl.BlockSpec((tm, D_MODEL), lambda i: (i, 0)),
        out_shape=jax.ShapeDtypeStruct((T, D_MODEL), f32),
        scratch_shapes=[pltpu.VMEM((tm, D_FF), bf16), pltpu.VMEM((HALO_ROWS, D_MODEL), f32)],
        compiler_params=_params(("arbitrary",)),
        name="post_mixer",
    )(*acts, w_mix, x, ln_g, ln_b, wq, kv, wo, wi, cw, cb, wf)


def _even_weight(w_in):
    qa, ka, va, ra, ga, qb, kb, vb = jnp.split(
        w_in, np.cumsum([256, 256, 512, 512, GLA_GATE_RANK, 512, 512, 512])[:-1].tolist(), axis=1)
    pad = jnp.zeros((D_MODEL, LANES - GLA_GATE_RANK), w_in.dtype)
    return jnp.concatenate(
        [qa * GLA_DK ** -0.5, ka, ra, va, qb * HEAD_DIM ** -0.5, kb, ga, pad, vb], axis=1).astype(bf16)


def _odd_weight(w_in):
    return jnp.concatenate([w_in[:, :1024] * HEAD_DIM ** -0.5, w_in[:, 1024:]], axis=1).astype(bf16)


def kernel(x, mem, positions, even_w_in, even_w_out, gla_gate_w, gla_gate_b, gla_norm_g, diff_lam_q1, diff_lam_k1, diff_lam_q2, diff_lam_k2, diff_norm_g, odd_w_in, odd_w_out, swa_sinks, mem_w_q, mem_w_kv, mem_w_out, ffn_w_in, ffn_conv_w, ffn_conv_b, ffn_w_out, ln_g, ln_b):
    B, S, D = x.shape
    T = B * S
    tm = min(512, S)
    xf = x.reshape(T, D)

    inv_freq = ROPE_THETA ** (-jnp.arange(0, HEAD_DIM, 2, dtype=f32) / HEAD_DIM)
    ang = positions.astype(f32).reshape(T, 1) * inv_freq
    cos, sin = jnp.cos(ang), jnp.sin(ang)
    cos_t = jnp.tile(cos, (1, 4))
    sin_t = jnp.concatenate([-sin, sin, -sin, sin], axis=1)

    memf = mem.reshape(B * mem.shape[1], D)
    row2 = lambda v: v.reshape(1, -1)

    for i in range(DEPTH):
        j = i // 2
        if i % 2 == 0:
            lam_init = 0.8 - 0.6 * float(np.exp(-0.3 * i))
            of, ob, vt = _even_in_proj(xf, _even_weight(even_w_in[j]), cos_t, sin_t, B, S, tm)
            gate_w = jnp.pad(gla_gate_w[j], ((0, LANES - GLA_GATE_RANK), (0, 0))).astype(bf16)
            o_a = _gla(of, ob, gate_w, row2(gla_gate_b[j]), row2(gla_norm_g[j]), B, S, min(4, S // CHUNK))
            o_b = _diff_attention(ob, vt, row2(diff_lam_q1[j]), row2(diff_lam_k1[j]), row2(diff_lam_q2[j]),
                                  row2(diff_lam_k2[j]), diff_norm_g[j].reshape(-1, 1), lam_init, B, S,
                                  min(1024, S))
            acts, w_mix = [o_a, o_b], even_w_out[j]
        else:
            qt, k, vt = _odd_in_proj(xf, _odd_weight(odd_w_in[j]), cos_t, sin_t, B, S, tm)
            sink_rows = jnp.repeat(swa_sinks[j].reshape(SWA_KV_HEADS, SWA_GROUP), SWA_PAIR, axis=1)
            acts, w_mix = [_swa(qt, k, vt, sink_rows, B, S, min(4, S // SWA_PAIR))], odd_w_out[j]

        kv = _matmul(memf, mem_w_kv[i].astype(bf16), memf.shape[0], bf16)
        xf = _post_mixer(acts, w_mix.astype(bf16), xf, ln_g[i], ln_b[i],
                         (mem_w_q[i] * MEM_DH ** -0.5).astype(bf16), kv, mem_w_out[i].astype(bf16),
                         ffn_w_in[i].astype(bf16), ffn_conv_w[i], row2(ffn_conv_b[i]), ffn_w_out[i].astype(bf16),
                         B, S, tm)
    return xf.reshape(B, S, D)
```

```python
import functools

import numpy as np
import jax
import jax.numpy as jnp
from jax import lax
from jax.experimental import pallas as pl
from jax.experimental.pallas import tpu as pltpu

f32 = jnp.float32
bf16 = jnp.bfloat16

D_MODEL = 1024
DEPTH = 4
CHUNK = 64
ROPE_THETA = 10000.0
LN_EPS = 1e-5
RMS_EPS = 1e-6
NEG_INF = -1e30
HEAD_DIM = 64

GLA_HEADS = 4
GLA_DK = 64
GLA_DV = 128
GLA_GATE_RANK = 16
GLA_TAU = 16.0

DIFF_HEADS = 4
DIFF_DV = 128

SWA_Q_HEADS = 16
SWA_KV_HEADS = 2
SWA_GROUP = SWA_Q_HEADS // SWA_KV_HEADS
WINDOW_CHUNKS = 2

MEM_HEADS = 4
MEM_DH = D_MODEL // MEM_HEADS

D_FF = 2816
DEEPNORM_ALPHA = (2 * DEPTH) ** 0.25
LOG2E = 1.4426950408889634

LANES = 128
VMEM_LIMIT = 56 * 1024 * 1024

_NT = (((1,), (1,)), ((), ()))
_TN = (((0,), (0,)), ((), ()))


def _params(semantics):
    return pltpu.CompilerParams(dimension_semantics=semantics, vmem_limit_bytes=VMEM_LIMIT)


def _dot(a, b):
    return jnp.dot(a, b, preferred_element_type=f32)


def _deepnorm_ln(x, h, g, b):
    y = DEEPNORM_ALPHA * x + h
    mu = jnp.mean(y, axis=-1, keepdims=True)
    d = y - mu
    var = jnp.mean(d * d, axis=-1, keepdims=True)
    return d * lax.rsqrt(var + LN_EPS) * g + b


def _rope(y, cos_t, sin_t):
    rows = y.shape[0]
    lane = lax.broadcasted_iota(jnp.int32, (rows, LANES), 1)
    first_half = (lane & 32) == 0
    outs = []
    for j in range(y.shape[1] // LANES):
        t = y[:, j * LANES:(j + 1) * LANES]
        swapped = jnp.where(first_half, pltpu.roll(t, LANES - 32, 1), pltpu.roll(t, 32, 1))
        outs.append(t * cos_t + swapped * sin_t)
    return jnp.concatenate(outs, axis=1)


EVEN_F32_COLS = 1024
EVEN_BF_COLS = 1664
DIFF_V_COLS = DIFF_HEADS * DIFF_DV


def _even_in_kernel(x_ref, w_ref, cos_ref, sin_ref, of_ref, ob_ref, vt_ref):
    xb = x_ref[...].astype(bf16)
    of_ref[...] = _dot(xb, w_ref[:, 0:1024])
    ob_ref[:, 0:512] = _dot(xb, w_ref[:, 1024:1536]).astype(bf16)
    c = cos_ref[...]
    s = sin_ref[...]
    ob_ref[:, 512:1024] = _rope(_dot(xb, w_ref[:, 1536:2048]), c * LOG2E, s * LOG2E).astype(bf16)
    ob_ref[:, 1024:1536] = _rope(_dot(xb, w_ref[:, 2048:2560]), c, s).astype(bf16)
    ob_ref[:, 1536:1664] = _dot(xb, w_ref[:, 2560:2688]).astype(bf16)
    vt_ref[...] = _dot(xb, w_ref[:, 2688:3200]).T.astype(bf16)


def _even_in_proj(x, w, cos_t, sin_t, B, S, tm):
    T = x.shape[0]
    nt = S // tm
    return pl.pallas_call(
        _even_in_kernel,
        grid=(T // tm,),
        in_specs=[
            pl.BlockSpec((tm, D_MODEL), lambda i: (i, 0)),
            pl.BlockSpec(w.shape, lambda i: (0, 0)),
            pl.BlockSpec((tm, LANES), lambda i: (i, 0)),
            pl.BlockSpec((tm, LANES), lambda i: (i, 0)),
        ],
        out_specs=[
            pl.BlockSpec((tm, EVEN_F32_COLS), lambda i: (i, 0)),
            pl.BlockSpec((tm, EVEN_BF_COLS), lambda i: (i, 0)),
            pl.BlockSpec((DIFF_V_COLS, tm), lambda i: (i // nt, i % nt)),
        ],
        out_shape=[
            jax.ShapeDtypeStruct((T, EVEN_F32_COLS), f32),
            jax.ShapeDtypeStruct((T, EVEN_BF_COLS), bf16),
            jax.ShapeDtypeStruct((B * DIFF_V_COLS, S), bf16),
        ],
        compiler_params=_params(("parallel",)),
        name="even_in_proj",
    )(x, w, cos_t, sin_t)


def _gla_kernel(of_ref, va_ref, ga_ref, gw_ref, gb_ref, ng_ref, o_ref, st_ref, *, nchunk):
    nb = of_ref.shape[0]
    tc = nchunk * CHUNK
    hk = GLA_HEADS * GLA_DK

    @pl.when(pl.program_id(0) == 0)
    def _():
        st_ref[...] = jnp.zeros_like(st_ref)

    row = lax.broadcasted_iota(jnp.int32, (tc, tc), 0)
    col = lax.broadcasted_iota(jnp.int32, (tc, tc), 1)
    causal = (col <= row) & ((col // CHUNK) == (row // CHUNK))
    tri = jnp.where(causal, 1.0, 0.0).astype(bf16)
    own_chunk = (lax.broadcasted_iota(jnp.int32, (tc, nchunk * GLA_DK), 0) // CHUNK
                 == lax.broadcasted_iota(jnp.int32, (tc, nchunk * GLA_DK), 1) // GLA_DK)
    gain = ng_ref[...]

    for bi in range(nb):
        z = _dot(ga_ref[bi], gw_ref[...]) + gb_ref[...]
        log_a = (jnp.minimum(z, 0.0) - jnp.log(1.0 + jnp.exp(-jnp.abs(z)))) * (1.0 / GLA_TAU)

        hi = log_a.astype(bf16)
        rem = log_a - hi.astype(f32)
        mid = rem.astype(bf16)
        lo = (rem - mid.astype(f32)).astype(bf16)
        b = _dot(tri, hi) + _dot(tri, mid) + _dot(tri, lo)

        b3 = b.reshape(nchunk, CHUNK, hk)
        b_last = b3[:, CHUNK - 1:CHUNK, :]
        k_carry = jnp.exp(b_last - b3).reshape(tc, hk)
        decay = jnp.exp(b_last)

        qa = of_ref[bi, :, 0:hk]
        ka = of_ref[bi, :, hk:2 * hk]
        q_dec = (qa * jnp.exp(b)).astype(bf16)
        k_neg = (ka * jnp.exp(-b)).astype(bf16)
        k_dec = (ka * k_carry).astype(bf16)

        for h in range(GLA_HEADS):
            ks = slice(h * GLA_DK, (h + 1) * GLA_DK)
            vs = slice(h * GLA_DV, (h + 1) * GLA_DV)
            qd = q_dec[:, ks]
            v = va_ref[bi, :, vs]
            a = lax.dot_general(qd, k_neg[:, ks], _NT, preferred_element_type=f32)
            o = _dot(jnp.where(causal, a, 0.0).astype(bf16), v)
            k_blocks = jnp.where(own_chunk, jnp.concatenate([k_dec[:, ks]] * nchunk, axis=1), 0.0)
            kv_t = lax.dot_general(v, k_blocks, _TN, preferred_element_type=f32)
            state_t = st_ref[bi, h]
            states = []
            for c in range(nchunk):
                states.append(state_t.astype(bf16))
                state_t = decay[c, :, ks] * state_t + kv_t[:, c * GLA_DK:(c + 1) * GLA_DK]
            st_ref[bi, h] = state_t
            read = lax.dot_general(qd, jnp.concatenate(states, axis=0), _NT,
                                   preferred_element_type=f32)
            o = o + jnp.concatenate([read[c * CHUNK:(c + 1) * CHUNK, c * GLA_DV:(c + 1) * GLA_DV]
                                     for c in range(nchunk)], axis=0)
            o = o * lax.rsqrt(jnp.mean(o * o, axis=-1, keepdims=True) + RMS_EPS) * gain
            rr = of_ref[bi, :, 2 * hk + h * GLA_DV:2 * hk + (h + 1) * GLA_DV]
            o_ref[bi, :, vs] = (o * (rr * (1.0 / (1.0 + jnp.exp(-rr))))).astype(bf16)


def _gla(of, ob, gate_w, gate_b, norm_g, B, S, nchunk):
    T = of.shape[0]
    tc = nchunk * CHUNK
    full = lambda a: pl.BlockSpec(a.shape, lambda i: (0, 0))
    out = pl.pallas_call(
        functools.partial(_gla_kernel, nchunk=nchunk),
        grid=(S // tc,),
        in_specs=[
            pl.BlockSpec((B, tc, EVEN_F32_COLS), lambda i: (0, i, 0)),
            pl.BlockSpec((B, tc, 512), lambda i: (0, i, 0)),
            pl.BlockSpec((B, tc, LANES), lambda i: (0, i, 1536 // LANES)),
            full(gate_w), full(gate_b), full(norm_g),
        ],
        out_specs=pl.BlockSpec((B, tc, GLA_HEADS * GLA_DV), lambda i: (0, i, 0)),
        out_shape=jax.ShapeDtypeStruct((B, S, GLA_HEADS * GLA_DV), bf16),
        scratch_shapes=[pltpu.VMEM((B, GLA_HEADS, GLA_DV, GLA_DK), f32)],
        compiler_params=_params(("arbitrary",)),
        name="gla",
    )(of.reshape(B, S, -1), ob.reshape(B, S, -1), ob.reshape(B, S, -1), gate_w, gate_b, norm_g)
    return out.reshape(T, -1)


def _diff_kernel(q_ref, k_ref, vt_ref, lq1_ref, lk1_ref, lq2_ref, lk2_ref, ng_ref, o_ref, s_ref, *, tq, tk,
                 lam_init):
    i = pl.program_id(2)
    q = q_ref[...]
    q_maps = (q[:, 0:HEAD_DIM], q[:, HEAD_DIM:2 * HEAD_DIM])

    def produce(j, slot, first_query=0):
        start = pl.multiple_of(j * tk, tk)
        k = k_ref[pl.ds(start, tk), :]
        col_max = []
        for t in range(2):
            s = lax.dot_general(k[:, t * HEAD_DIM:(t + 1) * HEAD_DIM], q_maps[t][first_query:, :], _NT,
                                preferred_element_type=f32)
            s_ref[slot, t, :, first_query:] = s
            col_max.append(jnp.max(s, axis=0, keepdims=True))
        return tuple(col_max)

    def softmax_step(s, vt, tile_max, m, l, acc):
        m_new = jnp.maximum(m, tile_max)
        alpha = jnp.exp2(m - m_new)
        p = jnp.exp2(s - m_new)
        l = alpha * l + jnp.sum(p, axis=0, keepdims=True)
        acc = alpha * acc + _dot(vt, p.astype(bf16))
        return [m_new, l, acc]

    def consume(j, slot, col_max, state, kind):
        start = pl.multiple_of(j * tk, tk)
        vt = vt_ref[:, pl.ds(start, tk)]
        if kind != "visible":
            diag = (lax.broadcasted_iota(jnp.int32, (tk, 1), 0) // CHUNK
                    <= lax.broadcasted_iota(jnp.int32, (1, tk), 1) // CHUNK)
        out = []
        for t in range(2):
            m, l, acc = state[3 * t:3 * t + 3]
            if kind == "visible":
                out += softmax_step(s_ref[slot, t], vt, col_max[t], m, l, acc)
            elif kind == "diag_first":
                s = s_ref[slot, t]
                s = jnp.concatenate([jnp.where(diag, s[:, :tk], NEG_INF), s[:, tk:]], axis=1)
                out += softmax_step(s, vt, jnp.max(s, axis=0, keepdims=True), m, l, acc)
            else:
                s = jnp.where(diag, s_ref[slot, t, :, tk:], NEG_INF)
                upd = softmax_step(s, vt, jnp.max(s, axis=0, keepdims=True), m[:, tk:], l[:, tk:], acc[:, tk:])
                out += [jnp.concatenate([old[:, :tk], new], axis=1) for old, new in zip((m, l, acc), upd)]
        return tuple(out)

    def pair(jj, carry):
        j = 2 * jj
        max1 = produce(j + 1, 1)
        state = consume(j, 0, carry[6:], carry[:6], "visible")
        max0 = produce(j + 2, 0)
        state = consume(j + 1, 1, max1, state, "visible")
        return state + max0

    init = (jnp.full((1, tq), NEG_INF, f32), jnp.zeros((1, tq), f32), jnp.zeros((DIFF_DV, tq), f32)) * 2
    carry = lax.fori_loop(0, i, pair, init + produce(0, 0))
    produce(2 * i + 1, 1, first_query=tk)
    state = consume(2 * i, 0, None, carry[:6], "diag_first")
    m0, l0, acc0, m1, l1, acc1 = consume(2 * i + 1, 1, None, state, "diag_second")

    lam = (jnp.exp(jnp.sum(lq1_ref[...] * lk1_ref[...], axis=1, keepdims=True))
           - jnp.exp(jnp.sum(lq2_ref[...] * lk2_ref[...], axis=1, keepdims=True)) + lam_init)
    o = acc0 * (1.0 / l0) - lam * (acc1 * (1.0 / l1))
    o = o * lax.rsqrt(jnp.mean(o * o, axis=0, keepdims=True) + RMS_EPS) * ng_ref[...]
    o_ref[...] = (o * (1.0 - lam_init)).T.astype(bf16)


def _diff_attention(ob, vt, lq1, lk1, lq2, lk2, norm_g, lam_init, B, S, tq):
    T = ob.shape[0]
    nq = S // tq
    tk = tq // 2
    small = lambda a: pl.BlockSpec(a.shape, lambda b, h, i: (0, 0))
    return pl.pallas_call(
        functools.partial(_diff_kernel, tq=tq, tk=tk, lam_init=lam_init),
        grid=(B, DIFF_HEADS, nq),
        in_specs=[
            pl.BlockSpec((tq, LANES), lambda b, h, i: (b * nq + i, 4 + h)),
            pl.BlockSpec((S, LANES), lambda b, h, i: (b, 8 + h)),
            pl.BlockSpec((DIFF_DV, S), lambda b, h, i: (b * DIFF_HEADS + h, 0)),
            small(lq1), small(lk1), small(lq2), small(lk2), small(norm_g),
        ],
        out_specs=pl.BlockSpec((tq, DIFF_DV), lambda b, h, i: (b * nq + i, h)),
        out_shape=jax.ShapeDtypeStruct((T, DIFF_HEADS * DIFF_DV), bf16),
        scratch_shapes=[pltpu.VMEM((2, 2, tk, tq), f32)],
        compiler_params=_params(("parallel", "parallel", "arbitrary")),
        name="diff_attention",
    )(ob, ob, vt, lq1, lk1, lq2, lk2, norm_g)


SWA_Q_COLS = SWA_Q_HEADS * HEAD_DIM
SWA_KV_COLS = SWA_KV_HEADS * HEAD_DIM
SWA_PAIR = 2 * CHUNK
SWA_BAND = (WINDOW_CHUNKS + 2) * CHUNK


def _odd_in_kernel(x_ref, w_ref, cos_ref, sin_ref, qt_ref, k_ref, vt_ref):
    xb = x_ref[...].astype(bf16)
    c = cos_ref[...]
    s = sin_ref[...]
    qt_ref[...] = _rope(_dot(xb, w_ref[:, 0:SWA_Q_COLS]), c * LOG2E, s * LOG2E).T.astype(bf16)
    kv = _dot(xb, w_ref[:, SWA_Q_COLS:SWA_Q_COLS + 2 * SWA_KV_COLS])
    k_ref[...] = _rope(kv[:, 0:SWA_KV_COLS], c, s).astype(bf16)
    vt_ref[...] = kv[:, SWA_KV_COLS:2 * SWA_KV_COLS].T.astype(bf16)


def _odd_in_proj(x, w, cos_t, sin_t, B, S, tm):
    T = x.shape[0]
    nt = S // tm
    return pl.pallas_call(
        _odd_in_kernel,
        grid=(T // tm,),
        in_specs=[
            pl.BlockSpec((tm, D_MODEL), lambda i: (i, 0)),
            pl.BlockSpec(w.shape, lambda i: (0, 0)),
            pl.BlockSpec((tm, LANES), lambda i: (i, 0)),
            pl.BlockSpec((tm, LANES), lambda i: (i, 0)),
        ],
        out_specs=[
            pl.BlockSpec((SWA_Q_COLS, tm), lambda i: (i // nt, i % nt)),
            pl.BlockSpec((tm, SWA_KV_COLS), lambda i: (i, 0)),
            pl.BlockSpec((SWA_KV_COLS, tm), lambda i: (i // nt, i % nt)),
        ],
        out_shape=[
            jax.ShapeDtypeStruct((B * SWA_Q_COLS, S), bf16),
            jax.ShapeDtypeStruct((T, SWA_KV_COLS), bf16),
            jax.ShapeDtypeStruct((B * SWA_KV_COLS, S), bf16),
        ],
        compiler_params=_params(("parallel",)),
        name="odd_in_proj",
    )(x, w, cos_t, sin_t)


def _swa_kernel(qt_ref, k_ref, vt_ref, sink_ref, o_ref, *, npair):
    i = pl.program_id(1)
    ncol = SWA_GROUP * SWA_PAIR
    key_chunk = lax.broadcasted_iota(jnp.int32, (SWA_BAND, ncol), 0) // CHUNK
    qry_chunk = (lax.broadcasted_iota(jnp.int32, (SWA_BAND, ncol), 1) // CHUNK) % 2
    bias_inner = jnp.where((key_chunk >= qry_chunk) & (key_chunk <= qry_chunk + WINDOW_CHUNKS), 0.0, NEG_INF)
    bias_first = jnp.where(key_chunk <= qry_chunk, 0.0, NEG_INF)
    for c in range(npair):
        pair = i * npair + c
        start = pl.multiple_of(jnp.maximum(pair - 1, 0) * SWA_PAIR, SWA_PAIR)
        kb = k_ref[pl.ds(start, SWA_BAND), :]
        vtb = vt_ref[:, pl.ds(start, SWA_BAND)]
        bias = jnp.where(i == 0, bias_first, bias_inner) if c == 0 else bias_inner
        rs = slice(c * SWA_PAIR, (c + 1) * SWA_PAIR)
        for j in range(SWA_KV_HEADS):
            hs = slice(j * HEAD_DIM, (j + 1) * HEAD_DIM)
            qt = jnp.concatenate(
                [qt_ref[(j * SWA_GROUP + g) * HEAD_DIM:(j * SWA_GROUP + g + 1) * HEAD_DIM, rs]
                 for g in range(SWA_GROUP)], axis=1)
            s = _dot(kb[:, hs], qt) + bias
            sink = sink_ref[j:j + 1, :] * LOG2E
            m = jnp.maximum(jnp.max(s, axis=0, keepdims=True), sink)
            p = jnp.exp2(s - m)
            den = jnp.sum(p, axis=0, keepdims=True) + jnp.exp2(sink - m)
            o = _dot(vtb[hs, :], p.astype(bf16)) * (1.0 / den)
            for gp in range(SWA_GROUP // 2):
                two_heads = jnp.concatenate([o[:, (2 * gp) * SWA_PAIR:(2 * gp + 1) * SWA_PAIR],
                                             o[:, (2 * gp + 1) * SWA_PAIR:(2 * gp + 2) * SWA_PAIR]], axis=0)
                col = (j * SWA_GROUP + 2 * gp) * HEAD_DIM
                o_ref[rs, col:col + 2 * HEAD_DIM] = two_heads.T.astype(bf16)


def _swa(qt, k, vt, sink_rows, B, S, npair):
    T = k.shape[0]
    tq = npair * SWA_PAIR
    nt = S // tq
    return pl.pallas_call(
        functools.partial(_swa_kernel, npair=npair),
        grid=(B, nt),
        in_specs=[
            pl.BlockSpec((SWA_Q_COLS, tq), lambda b, i: (b, i)),
            pl.BlockSpec((S, SWA_KV_COLS), lambda b, i: (b, 0)),
            pl.BlockSpec((SWA_KV_COLS, S), lambda b, i: (b, 0)),
            pl.BlockSpec(sink_rows.shape, lambda b, i: (0, 0)),
        ],
        out_specs=pl.BlockSpec((tq, SWA_Q_COLS), lambda b, i: (b * nt + i, 0)),
        out_shape=jax.ShapeDtypeStruct((T, SWA_Q_COLS), bf16),
        compiler_params=_params(("parallel", "arbitrary")),
        name="swa",
    )(qt, k, vt, sink_rows)


def _matmul_kernel(a_ref, w_ref, o_ref):
    o_ref[...] = _dot(a_ref[...].astype(bf16), w_ref[...]).astype(o_ref.dtype)


def _matmul(a, w, tm, out_dtype):
    M, K = a.shape
    N = w.shape[1]
    return pl.pallas_call(
        _matmul_kernel,
        grid=(M // tm,),
        in_specs=[pl.BlockSpec((tm, K), lambda i: (i, 0)), pl.BlockSpec((K, N), lambda i: (0, 0))],
        out_specs=pl.BlockSpec((tm, N), lambda i: (i, 0)),
        out_shape=jax.ShapeDtypeStruct((M, N), out_dtype),
        compiler_params=_params(("parallel",)),
        name="matmul",
    )(a, w)


FFN_COL_TILE = 256
HALO_ROWS = 8
GELU_C = float(np.sqrt(2.0 / np.pi))


def _mem_attention_block(x, wq_ref, kv_ref, wo_ref):
    q = _dot(x.astype(bf16), wq_ref[...]).astype(bf16)
    heads = []
    for h in range(MEM_HEADS):
        hs = slice(h * MEM_DH, (h + 1) * MEM_DH)
        s = lax.dot_general(q[:, hs], kv_ref[:, hs], _NT, preferred_element_type=f32)
        p = jnp.exp(s - jnp.max(s, axis=1, keepdims=True))
        inv = 1.0 / jnp.sum(p, axis=1, keepdims=True)
        o = _dot(p.astype(bf16), kv_ref[:, D_MODEL + h * MEM_DH:D_MODEL + (h + 1) * MEM_DH]) * inv
        heads.append(o.astype(bf16))
    return _dot(jnp.concatenate(heads, axis=1), wo_ref[...])


def _conv_ffn_block(x, x_prev, wi_ref, cw_ref, cb_ref, wo_ref, h_ref):
    tm = x.shape[0]
    xb = x.astype(bf16)
    xpb = x_prev.astype(bf16)
    row = lax.broadcasted_iota(jnp.int32, (HALO_ROWS, FFN_COL_TILE), 0)
    for f in range(D_FF // FFN_COL_TILE):
        cs = slice(f * FFN_COL_TILE, (f + 1) * FFN_COL_TILE)
        g = _dot(xb, wi_ref[:, cs])
        u = _dot(xb, wi_ref[:, D_FF + f * FFN_COL_TILE:D_FF + (f + 1) * FFN_COL_TILE])
        gp = _dot(xpb, wi_ref[:, cs])
        g1 = pltpu.roll(g, 1, 0)
        g2 = pltpu.roll(g, 2, 0)
        last, last2 = gp[HALO_ROWS - 1:HALO_ROWS, :], gp[HALO_ROWS - 2:HALO_ROWS - 1, :]
        g1 = jnp.concatenate([jnp.where(row < 1, last, g1[:HALO_ROWS, :]), g1[HALO_ROWS:, :]], axis=0)
        g2 = jnp.concatenate([jnp.where(row < 2, jnp.where(row < 1, last2, last), g2[:HALO_ROWS, :]),
                              g2[HALO_ROWS:, :]], axis=0)
        y = cw_ref[0:1, cs] * g2 + cw_ref[1:2, cs] * g1 + cw_ref[2:3, cs] * g + cb_ref[:, cs]
        inner = y * (GELU_C + (0.044715 * GELU_C) * (y * y))
        h_ref[:, cs] = ((0.5 * y) * (1.0 + jnp.tanh(inner)) * u).astype(bf16)
    return _dot(h_ref[...], wo_ref[...])


def _post_kernel(*refs, n_act, tiles_per_seq):
    acts = refs[:n_act]
    (w_mix_ref, x_ref, lng_ref, lnb_ref, wq_ref, kv_ref, wo_ref, wi_ref, cw_ref, cb_ref, wf_ref,
     o_ref, h_ref, halo_ref) = refs[n_act:]
    tm = x_ref.shape[0]
    i = pl.program_id(0)

    @pl.when(i == 0)
    def _():
        halo_ref[...] = jnp.zeros_like(halo_ref)

    h = None
    off = 0
    for a_ref in acts:
        k = a_ref.shape[1]
        part = _dot(a_ref[...], w_mix_ref[off:off + k, :])
        h = part if h is None else h + part
        off += k
    x = _deepnorm_ln(x_ref[...], h, lng_ref[0:1, :], lnb_ref[0:1, :])
    x = _deepnorm_ln(x, _mem_attention_block(x, wq_ref, kv_ref, wo_ref), lng_ref[1:2, :], lnb_ref[1:2, :])
    x_prev = jnp.where((i % tiles_per_seq) != 0, halo_ref[...], 0.0)
    halo_ref[...] = x[tm - HALO_ROWS:, :]
    h = _conv_ffn_block(x, x_prev, wi_ref, cw_ref, cb_ref, wf_ref, h_ref)
    o_ref[...] = _deepnorm_ln(x, h, lng_ref[2:3, :], lnb_ref[2:3, :])


def _post_mixer(acts, w_mix, x, ln_g, ln_b, wq, kv, wo, wi, cw, cb, wf, B, S, tm):
    T = x.shape[0]
    tiles_per_seq = S // tm
    M = kv.shape[0] // B
    resident = lambda a: pl.BlockSpec(a.shape, lambda i: (0, 0), pipeline_mode=pl.Buffered(1))
    small = lambda a: pl.BlockSpec(a.shape, lambda i: (0, 0))
    return pl.pallas_call(
        functools.partial(_post_kernel, n_act=len(acts), tiles_per_seq=tiles_per_seq),
        grid=(T // tm,),
        in_specs=[pl.BlockSpec((tm, a.shape[1]), lambda i: (i, 0)) for a in acts] + [
            resident(w_mix),
            pl.BlockSpec((tm, D_MODEL), lambda i: (i, 0)),
            small(ln_g), small(ln_b),
            resident(wq),
            pl.BlockSpec((M, 2 * D_MODEL), lambda i: (i // tiles_per_seq, 0)),
            resident(wo), resident(wi), small(cw), small(cb), resident(wf),
        ],
        out_specs=pl.BlockSpec((tm, D_MODEL), lambda i: (i, 0)),
        out_shape=jax.ShapeDtypeStruct((T, D_MODEL), f32),
        scratch_shapes=[pltpu.VMEM((tm, D_FF), bf16), pltpu.VMEM((HALO_ROWS, D_MODEL), f32)],
        compiler_params=_params(("arbitrary",)),
        name="post_mixer",
    )(*acts, w_mix, x, ln_g, ln_b, wq, kv, wo, wi, cw, cb, wf)


def _even_weight(w_in):
    qa, ka, va, ra, ga, qb, kb, vb = jnp.split(
        w_in, np.cumsum([256, 256, 512, 512, GLA_GATE_RANK, 512, 512, 512])[:-1].tolist(), axis=1)
    pad = jnp.zeros((D_MODEL, LANES - GLA_GATE_RANK), w_in.dtype)
    return jnp.concatenate(
        [qa * GLA_DK ** -0.5, ka, ra, va, qb * HEAD_DIM ** -0.5, kb, ga, pad, vb], axis=1).astype(bf16)


def _odd_weight(w_in):
    return jnp.concatenate([w_in[:, :1024] * HEAD_DIM ** -0.5, w_in[:, 1024:]], axis=1).astype(bf16)


def kernel(x, mem, positions, even_w_in, even_w_out, gla_gate_w, gla_gate_b, gla_norm_g, diff_lam_q1, diff_lam_k1, diff_lam_q2, diff_lam_k2, diff_norm_g, odd_w_in, odd_w_out, swa_sinks, mem_w_q, mem_w_kv, mem_w_out, ffn_w_in, ffn_conv_w, ffn_conv_b, ffn_w_out, ln_g, ln_b):
    B, S, D = x.shape
    T = B * S
    tm = min(512, S)
    xf = x.reshape(T, D)

    inv_freq = ROPE_THETA ** (-jnp.arange(0, HEAD_DIM, 2, dtype=f32) / HEAD_DIM)
    ang = positions.astype(f32).reshape(T, 1) * inv_freq
    cos, sin = jnp.cos(ang), jnp.sin(ang)
    cos_t = jnp.tile(cos, (1, 4))
    sin_t = jnp.concatenate([-sin, sin, -sin, sin], axis=1)

    memf = mem.reshape(B * mem.shape[1], D)
    row2 = lambda v: v.reshape(1, -1)

    for i in range(DEPTH):
        j = i // 2
        if i % 2 == 0:
            lam_init = 0.8 - 0.6 * float(np.exp(-0.3 * i))
            of, ob, vt = _even_in_proj(xf, _even_weight(even_w_in[j]), cos_t, sin_t, B, S, tm)
            gate_w = jnp.pad(gla_gate_w[j], ((0, LANES - GLA_GATE_RANK), (0, 0))).astype(bf16)
            o_a = _gla(of, ob, gate_w, row2(gla_gate_b[j]), row2(gla_norm_g[j]), B, S, min(4, S // CHUNK))
            o_b = _diff_attention(ob, vt, row2(diff_lam_q1[j]), row2(diff_lam_k1[j]), row2(diff_lam_q2[j]),
                                  row2(diff_lam_k2[j]), diff_norm_g[j].reshape(-1, 1), lam_init, B, S,
                                  min(1024, S))
            acts, w_mix = [o_a, o_b], even_w_out[j]
        else:
            qt, k, vt = _odd_in_proj(xf, _odd_weight(odd_w_in[j]), cos_t, sin_t, B, S, tm)
            sink_rows = jnp.repeat(swa_sinks[j].reshape(SWA_KV_HEADS, SWA_GROUP), SWA_PAIR, axis=1)
            acts, w_mix = [_swa(qt, k, vt, sink_rows, B, S, min(4, S // SWA_PAIR))], odd_w_out[j]

        kv = _matmul(memf, mem_w_kv[i].astype(bf16), memf.shape[0], bf16)
        xf = _post_mixer(acts, w_mix.astype(bf16), xf, ln_g[i], ln_b[i],
                         (mem_w_q[i] * MEM_DH ** -0.5).astype(bf16), kv, mem_w_out[i].astype(bf16),
                         ffn_w_in[i].astype(bf16), ffn_conv_w[i], row2(ffn_conv_b[i]), ffn_w_out[i].astype(bf16),
                         B, S, tm)
    return xf.reshape(B, S, D)
```

```python
import functools

import numpy as np
import jax
import jax.numpy as jnp
from jax import lax
from jax.experimental import pallas as pl
from jax.experimental.pallas import tpu as pltpu

f32 = jnp.float32
bf16 = jnp.bfloat16

D_MODEL = 1024
DEPTH = 4
CHUNK = 64
ROPE_THETA = 10000.0
LN_EPS = 1e-5
RMS_EPS = 1e-6
NEG_INF = -1e30
HEAD_DIM = 64

GLA_HEADS = 4
GLA_DK = 64
GLA_DV = 128
GLA_GATE_RANK = 16
GLA_TAU = 16.0

DIFF_HEADS = 4
DIFF_DV = 128

SWA_Q_HEADS = 16
SWA_KV_HEADS = 2
SWA_GROUP = SWA_Q_HEADS // SWA_KV_HEADS
WINDOW_CHUNKS = 2

MEM_HEADS = 4
MEM_DH = D_MODEL // MEM_HEADS

D_FF = 2816
DEEPNORM_ALPHA = (2 * DEPTH) ** 0.25
LOG2E = 1.4426950408889634

LANES = 128
VMEM_LIMIT = 56 * 1024 * 1024

_NT = (((1,), (1,)), ((), ()))
_TN = (((0,), (0,)), ((), ()))


def _params(semantics):
    return pltpu.CompilerParams(dimension_semantics=semantics, vmem_limit_bytes=VMEM_LIMIT)


def _dot(a, b):
    return jnp.dot(a, b, preferred_element_type=f32)


def _deepnorm_ln(x, h, g, b):
    y = DEEPNORM_ALPHA * x + h
    mu = jnp.mean(y, axis=-1, keepdims=True)
    d = y - mu
    var = jnp.mean(d * d, axis=-1, keepdims=True)
    return d * lax.rsqrt(var + LN_EPS) * g + b


def _rope(y, cos_t, sin_t):
    rows = y.shape[0]
    lane = lax.broadcasted_iota(jnp.int32, (rows, LANES), 1)
    first_half = (lane & 32) == 0
    outs = []
    for j in range(y.shape[1] // LANES):
        t = y[:, j * LANES:(j + 1) * LANES]
        swapped = jnp.where(first_half, pltpu.roll(t, LANES - 32, 1), pltpu.roll(t, 32, 1))
        outs.append(t * cos_t + swapped * sin_t)
    return jnp.concatenate(outs, axis=1)


EVEN_F32_COLS = 1024
EVEN_BF_COLS = 1664
DIFF_V_COLS = DIFF_HEADS * DIFF_DV


def _even_in_kernel(x_ref, w_ref, cos_ref, sin_ref, of_ref, ob_ref, vt_ref):
    xb = x_ref[...].astype(bf16)
    of_ref[...] = _dot(xb, w_ref[:, 0:1024])
    ob_ref[:, 0:512] = _dot(xb, w_ref[:, 1024:1536]).astype(bf16)
    c = cos_ref[...]
    s = sin_ref[...]
    ob_ref[:, 512:1024] = _rope(_dot(xb, w_ref[:, 1536:2048]), c * LOG2E, s * LOG2E).astype(bf16)
    ob_ref[:, 1024:1536] = _rope(_dot(xb, w_ref[:, 2048:2560]), c, s).astype(bf16)
    ob_ref[:, 1536:1664] = _dot(xb, w_ref[:, 2560:2688]).astype(bf16)
    vt_ref[...] = _dot(xb, w_ref[:, 2688:3200]).T.astype(bf16)


def _even_in_proj(x, w, cos_t, sin_t, B, S, tm):
    T = x.shape[0]
    nt = S // tm
    return pl.pallas_call(
        _even_in_kernel,
        grid=(T // tm,),
        in_specs=[
            pl.BlockSpec((tm, D_MODEL), lambda i: (i, 0)),
            pl.BlockSpec(w.shape, lambda i: (0, 0)),
            pl.BlockSpec((tm, LANES), lambda i: (i, 0)),
            pl.BlockSpec((tm, LANES), lambda i: (i, 0)),
        ],
        out_specs=[
            pl.BlockSpec((tm, EVEN_F32_COLS), lambda i: (i, 0)),
            pl.BlockSpec((tm, EVEN_BF_COLS), lambda i: (i, 0)),
            pl.BlockSpec((DIFF_V_COLS, tm), lambda i: (i // nt, i % nt)),
        ],
        out_shape=[
            jax.ShapeDtypeStruct((T, EVEN_F32_COLS), f32),
            jax.ShapeDtypeStruct((T, EVEN_BF_COLS), bf16),
            jax.ShapeDtypeStruct((B * DIFF_V_COLS, S), bf16),
        ],
        compiler_params=_params(("parallel",)),
        name="even_in_proj",
    )(x, w, cos_t, sin_t)


def _gla_kernel(of_ref, va_ref, ga_ref, gw_ref, gb_ref, ng_ref, o_ref, st_ref, *, nchunk):
    nb = of_ref.shape[0]
    tc = nchunk * CHUNK
    hk = GLA_HEADS * GLA_DK

    @pl.when(pl.program_id(0) == 0)
    def _():
        st_ref[...] = jnp.zeros_like(st_ref)

    row = lax.broadcasted_iota(jnp.int32, (tc, tc), 0)
    col = lax.broadcasted_iota(jnp.int32, (tc, tc), 1)
    causal = (col <= row) & ((col // CHUNK) == (row // CHUNK))
    tri = jnp.where(causal, 1.0, 0.0).astype(bf16)
    own_chunk = (lax.broadcasted_iota(jnp.int32, (tc, nchunk * GLA_DK), 0) // CHUNK
                 == lax.broadcasted_iota(jnp.int32, (tc, nchunk * GLA_DK), 1) // GLA_DK)
    gain = ng_ref[...]

    for bi in range(nb):
        z = _dot(ga_ref[bi], gw_ref[...]) + gb_ref[...]
        log_a = (jnp.minimum(z, 0.0) - jnp.log(1.0 + jnp.exp(-jnp.abs(z)))) * (1.0 / GLA_TAU)

        hi = log_a.astype(bf16)
        rem = log_a - hi.astype(f32)
        mid = rem.astype(bf16)
        lo = (rem - mid.astype(f32)).astype(bf16)
        b = _dot(tri, hi) + _dot(tri, mid) + _dot(tri, lo)

        b3 = b.reshape(nchunk, CHUNK, hk)
        b_last = b3[:, CHUNK - 1:CHUNK, :]
        k_carry = jnp.exp(b_last - b3).reshape(tc, hk)
        decay = jnp.exp(b_last)

        qa = of_ref[bi, :, 0:hk]
        ka = of_ref[bi, :, hk:2 * hk]
        q_dec = (qa * jnp.exp(b)).astype(bf16)
        k_neg = (ka * jnp.exp(-b)).astype(bf16)
        k_dec = (ka * k_carry).astype(bf16)

        for h in range(GLA_HEADS):
            ks = slice(h * GLA_DK, (h + 1) * GLA_DK)
            vs = slice(h * GLA_DV, (h + 1) * GLA_DV)
            qd = q_dec[:, ks]
            v = va_ref[bi, :, vs]
            a = lax.dot_general(qd, k_neg[:, ks], _NT, preferred_element_type=f32)
            o = _dot(jnp.where(causal, a, 0.0).astype(bf16), v)
            k_blocks = jnp.where(own_chunk, jnp.concatenate([k_dec[:, ks]] * nchunk, axis=1), 0.0)
            kv_t = lax.dot_general(v, k_blocks, _TN, preferred_element_type=f32)
            state_t = st_ref[bi, h]
            states = []
            for c in range(nchunk):
                states.append(state_t.astype(bf16))
                state_t = decay[c, :, ks] * state_t + kv_t[:, c * GLA_DK:(c + 1) * GLA_DK]
            st_ref[bi, h] = state_t
            read = lax.dot_general(qd, jnp.concatenate(states, axis=0), _NT,
                                   preferred_element_type=f32)
            o = o + jnp.concatenate([read[c * CHUNK:(c + 1) * CHUNK, c * GLA_DV:(c + 1) * GLA_DV]
                                     for c in range(nchunk)], axis=0)
            o = o * lax.rsqrt(jnp.mean(o * o, axis=-1, keepdims=True) + RMS_EPS) * gain
            rr = of_ref[bi, :, 2 * hk + h * GLA_DV:2 * hk + (h + 1) * GLA_DV]
            o_ref[bi, :, vs] = (o * (rr * (1.0 / (1.0 + jnp.exp(-rr))))).astype(bf16)


def _gla(of, ob, gate_w, gate_b, norm_g, B, S, nchunk):
    T = of.shape[0]
    tc = nchunk * CHUNK
    full = lambda a: pl.BlockSpec(a.shape, lambda i: (0, 0))
    out = pl.pallas_call(
        functools.partial(_gla_kernel, nchunk=nchunk),
        grid=(S // tc,),
        in_specs=[
            pl.BlockSpec((B, tc, EVEN_F32_COLS), lambda i: (0, i, 0)),
            pl.BlockSpec((B, tc, 512), lambda i: (0, i, 0)),
            pl.BlockSpec((B, tc, LANES), lambda i: (0, i, 1536 // LANES)),
            full(gate_w), full(gate_b), full(norm_g),
        ],
        out_specs=pl.BlockSpec((B, tc, GLA_HEADS * GLA_DV), lambda i: (0, i, 0)),
        out_shape=jax.ShapeDtypeStruct((B, S, GLA_HEADS * GLA_DV), bf16),
        scratch_shapes=[pltpu.VMEM((B, GLA_HEADS, GLA_DV, GLA_DK), f32)],
        compiler_params=_params(("arbitrary",)),
        name="gla",
    )(of.reshape(B, S, -1), ob.reshape(B, S, -1), ob.reshape(B, S, -1), gate_w, gate_b, norm_g)
    return out.reshape(T, -1)


def _diff_kernel(q_ref, k_ref, vt_ref, lq1_ref, lk1_ref, lq2_ref, lk2_ref, ng_ref, o_ref, s_ref, *, tq, tk,
                 lam_init):
    i = pl.program_id(2)
    q = q_ref[...]
    q_maps = (q[:, 0:HEAD_DIM], q[:, HEAD_DIM:2 * HEAD_DIM])

    def produce(j, slot, first_query=0):
        start = pl.multiple_of(j * tk, tk)
        k = k_ref[pl.ds(start, tk), :]
        col_max = []
        for t in range(2):
            s = lax.dot_general(k[:, t * HEAD_DIM:(t + 1) * HEAD_DIM], q_maps[t][first_query:, :], _NT,
                                preferred_element_type=f32)
            s_ref[slot, t, :, first_query:] = s
            col_max.append(jnp.max(s, axis=0, keepdims=True))
        return tuple(col_max)

    def softmax_step(s, vt, tile_max, m, l, acc):
        m_new = jnp.maximum(m, tile_max)
        alpha = jnp.exp2(m - m_new)
        p = jnp.exp2(s - m_new)
        l = alpha * l + jnp.sum(p, axis=0, keepdims=True)
        acc = alpha * acc + _dot(vt, p.astype(bf16))
        return [m_new, l, acc]

    def consume(j, slot, col_max, state, kind):
        start = pl.multiple_of(j * tk, tk)
        vt = vt_ref[:, pl.ds(start, tk)]
        if kind != "visible":
            diag = (lax.broadcasted_iota(jnp.int32, (tk, 1), 0) // CHUNK
                    <= lax.broadcasted_iota(jnp.int32, (1, tk), 1) // CHUNK)
        out = []
        for t in range(2):
            m, l, acc = state[3 * t:3 * t + 3]
            if kind == "visible":
                out += softmax_step(s_ref[slot, t], vt, col_max[t], m, l, acc)
            elif kind == "diag_first":
                s = s_ref[slot, t]
                s = jnp.concatenate([jnp.where(diag, s[:, :tk], NEG_INF), s[:, tk:]], axis=1)
                out += softmax_step(s, vt, jnp.max(s, axis=0, keepdims=True), m, l, acc)
            else:
                s = jnp.where(diag, s_ref[slot, t, :, tk:], NEG_INF)
                upd = softmax_step(s, vt, jnp.max(s, axis=0, keepdims=True), m[:, tk:], l[:, tk:], acc[:, tk:])
                out += [jnp.concatenate([old[:, :tk], new], axis=1) for old, new in zip((m, l, acc), upd)]
        return tuple(out)

    def pair(jj, carry):
        j = 2 * jj
        max1 = produce(j + 1, 1)
        state = consume(j, 0, carry[6:], carry[:6], "visible")
        max0 = produce(j + 2, 0)
        state = consume(j + 1, 1, max1, state, "visible")
        return state + max0

    init = (jnp.full((1, tq), NEG_INF, f32), jnp.zeros((1, tq), f32), jnp.zeros((DIFF_DV, tq), f32)) * 2
    carry = lax.fori_loop(0, i, pair, init + produce(0, 0))
    produce(2 * i + 1, 1, first_query=tk)
    state = consume(2 * i, 0, None, carry[:6], "diag_first")
    m0, l0, acc0, m1, l1, acc1 = consume(2 * i + 1, 1, None, state, "diag_second")

    lam = (jnp.exp(jnp.sum(lq1_ref[...] * lk1_ref[...], axis=1, keepdims=True))
           - jnp.exp(jnp.sum(lq2_ref[...] * lk2_ref[...], axis=1, keepdims=True)) + lam_init)
    o = acc0 * (1.0 / l0) - lam * (acc1 * (1.0 / l1))
    o = o * lax.rsqrt(jnp.mean(o * o, axis=0, keepdims=True) + RMS_EPS) * ng_ref[...]
    o_ref[...] = (o * (1.0 - lam_init)).T.astype(bf16)


def _diff_attention(ob, vt, lq1, lk1, lq2, lk2, norm_g, lam_init, B, S, tq):
    T = ob.shape[0]
    nq = S // tq
    tk = tq // 2
    small = lambda a: pl.BlockSpec(a.shape, lambda b, h, i: (0, 0))
    return pl.pallas_call(
        functools.partial(_diff_kernel, tq=tq, tk=tk, lam_init=lam_init),
        grid=(B, DIFF_HEADS, nq),
        in_specs=[
            pl.BlockSpec((tq, LANES), lambda b, h, i: (b * nq + i, 4 + h)),
            pl.BlockSpec((S, LANES), lambda b, h, i: (b, 8 + h)),
            pl.BlockSpec((DIFF_DV, S), lambda b, h, i: (b * DIFF_HEADS + h, 0)),
            small(lq1), small(lk1), small(lq2), small(lk2), small(norm_g),
        ],
        out_specs=pl.BlockSpec((tq, DIFF_DV), lambda b, h, i: (b * nq + i, h)),
        out_shape=jax.ShapeDtypeStruct((T, DIFF_HEADS * DIFF_DV), bf16),
        scratch_shapes=[pltpu.VMEM((2, 2, tk, tq), f32)],
        compiler_params=_params(("parallel", "parallel", "arbitrary")),
        name="diff_attention",
    )(ob, ob, vt, lq1, lk1, lq2, lk2, norm_g)


SWA_Q_COLS = SWA_Q_HEADS * HEAD_DIM
SWA_KV_COLS = SWA_KV_HEADS * HEAD_DIM
SWA_PAIR = 2 * CHUNK
SWA_BAND = (WINDOW_CHUNKS + 2) * CHUNK


def _odd_in_kernel(x_ref, w_ref, cos_ref, sin_ref, qt_ref, k_ref, vt_ref):
    xb = x_ref[...].astype(bf16)
    c = cos_ref[...]
    s = sin_ref[...]
    qt_ref[...] = _rope(_dot(xb, w_ref[:, 0:SWA_Q_COLS]), c * LOG2E, s * LOG2E).T.astype(bf16)
    kv = _dot(xb, w_ref[:, SWA_Q_COLS:SWA_Q_COLS + 2 * SWA_KV_COLS])
    k_ref[...] = _rope(kv[:, 0:SWA_KV_COLS], c, s).astype(bf16)
    vt_ref[...] = kv[:, SWA_KV_COLS:2 * SWA_KV_COLS].T.astype(bf16)


def _odd_in_proj(x, w, cos_t, sin_t, B, S, tm):
    T = x.shape[0]
    nt = S // tm
    return pl.pallas_call(
        _odd_in_kernel,
        grid=(T // tm,),
        in_specs=[
            pl.BlockSpec((tm, D_MODEL), lambda i: (i, 0)),
            pl.BlockSpec(w.shape, lambda i: (0, 0)),
            pl.BlockSpec((tm, LANES), lambda i: (i, 0)),
            pl.BlockSpec((tm, LANES), lambda i: (i, 0)),
        ],
        out_specs=[
            pl.BlockSpec((SWA_Q_COLS, tm), lambda i: (i // nt, i % nt)),
            pl.BlockSpec((tm, SWA_KV_COLS), lambda i: (i, 0)),
            pl.BlockSpec((SWA_KV_COLS, tm), lambda i: (i // nt, i % nt)),
        ],
        out_shape=[
            jax.ShapeDtypeStruct((B * SWA_Q_COLS, S), bf16),
            jax.ShapeDtypeStruct((T, SWA_KV_COLS), bf16),
            jax.ShapeDtypeStruct((B * SWA_KV_COLS, S), bf16),
        ],
        compiler_params=_params(("parallel",)),
        name="odd_in_proj",
    )(x, w, cos_t, sin_t)


def _swa_kernel(qt_ref, k_ref, vt_ref, sink_ref, o_ref, *, npair):
    i = pl.program_id(1)
    ncol = SWA_GROUP * SWA_PAIR
    key_chunk = lax.broadcasted_iota(jnp.int32, (SWA_BAND, ncol), 0) // CHUNK
    qry_chunk = (lax.broadcasted_iota(jnp.int32, (SWA_BAND, ncol), 1) // CHUNK) % 2
    bias_inner = jnp.where((key_chunk >= qry_chunk) & (key_chunk <= qry_chunk + WINDOW_CHUNKS), 0.0, NEG_INF)
    bias_first = jnp.where(key_chunk <= qry_chunk, 0.0, NEG_INF)
    for c in range(npair):
        pair = i * npair + c
        start = pl.multiple_of(jnp.maximum(pair - 1, 0) * SWA_PAIR, SWA_PAIR)
        kb = k_ref[pl.ds(start, SWA_BAND), :]
        vtb = vt_ref[:, pl.ds(start, SWA_BAND)]
        bias = jnp.where(i == 0, bias_first, bias_inner) if c == 0 else bias_inner
        rs = slice(c * SWA_PAIR, (c + 1) * SWA_PAIR)
        for j in range(SWA_KV_HEADS):
            hs = slice(j * HEAD_DIM, (j + 1) * HEAD_DIM)
            qt = jnp.concatenate(
                [qt_ref[(j * SWA_GROUP + g) * HEAD_DIM:(j * SWA_GROUP + g + 1) * HEAD_DIM, rs]
                 for g in range(SWA_GROUP)], axis=1)
            s = _dot(kb[:, hs], qt) + bias
            sink = sink_ref[j:j + 1, :] * LOG2E
            m = jnp.maximum(jnp.max(s, axis=0, keepdims=True), sink)
            p = jnp.exp2(s - m)
            den = jnp.sum(p, axis=0, keepdims=True) + jnp.exp2(sink - m)
            o = _dot(vtb[hs, :], p.astype(bf16)) * (1.0 / den)
            for gp in range(SWA_GROUP // 2):
                two_heads = jnp.concatenate([o[:, (2 * gp) * SWA_PAIR:(2 * gp + 1) * SWA_PAIR],
                                             o[:, (2 * gp + 1) * SWA_PAIR:(2 * gp + 2) * SWA_PAIR]], axis=0)
                col = (j * SWA_GROUP + 2 * gp) * HEAD_DIM
                o_ref[rs, col:col + 2 * HEAD_DIM] = two_heads.T.astype(bf16)


def _swa(qt, k, vt, sink_rows, B, S, npair):
    T = k.shape[0]
    tq = npair * SWA_PAIR
    nt = S // tq
    return pl.pallas_call(
        functools.partial(_swa_kernel, npair=npair),
        grid=(B, nt),
        in_specs=[
            pl.BlockSpec((SWA_Q_COLS, tq), lambda b, i: (b, i)),
            pl.BlockSpec((S, SWA_KV_COLS), lambda b, i: (b, 0)),
            pl.BlockSpec((SWA_KV_COLS, S), lambda b, i: (b, 0)),
            pl.BlockSpec(sink_rows.shape, lambda b, i: (0, 0)),
        ],
        out_specs=pl.BlockSpec((tq, SWA_Q_COLS), lambda b, i: (b * nt + i, 0)),
        out_shape=jax.ShapeDtypeStruct((T, SWA_Q_COLS), bf16),
        compiler_params=_params(("parallel", "arbitrary")),
        name="swa",
    )(qt, k, vt, sink_rows)


def _matmul_kernel(a_ref, w_ref, o_ref):
    o_ref[...] = _dot(a_ref[...].astype(bf16), w_ref[...]).astype(o_ref.dtype)


def _matmul(a, w, tm, out_dtype):
    M, K = a.shape
    N = w.shape[1]
    return pl.pallas_call(
        _matmul_kernel,
        grid=(M // tm,),
        in_specs=[pl.BlockSpec((tm, K), lambda i: (i, 0)), pl.BlockSpec((K, N), lambda i: (0, 0))],
        out_specs=pl.BlockSpec((tm, N), lambda i: (i, 0)),
        out_shape=jax.ShapeDtypeStruct((M, N), out_dtype),
        compiler_params=_params(("parallel",)),
        name="matmul",
    )(a, w)


FFN_COL_TILE = 256
HALO_ROWS = 16
GELU_C = float(np.sqrt(2.0 / np.pi))


def _mem_attention_block(x, wq_ref, kv_ref, wo_ref):
    q = _dot(x.astype(bf16), wq_ref[...]).astype(bf16)
    heads = []
    for h in range(MEM_HEADS):
        hs = slice(h * MEM_DH, (h + 1) * MEM_DH)
        s = lax.dot_general(q[:, hs], kv_ref[:, hs], _NT, preferred_element_type=f32)
        p = jnp.exp(s - jnp.max(s, axis=1, keepdims=True))
        inv = 1.0 / jnp.sum(p, axis=1, keepdims=True)
        o = _dot(p.astype(bf16), kv_ref[:, D_MODEL + h * MEM_DH:D_MODEL + (h + 1) * MEM_DH]) * inv
        heads.append(o.astype(bf16))
    return _dot(jnp.concatenate(heads, axis=1), wo_ref[...])


def _conv_ffn_block(x, x_prev, wi_ref, cw_ref, cb_ref, wo_ref, h_ref):
    tm = x.shape[0]
    xb = x.astype(bf16)
    x_all = jnp.concatenate([x_prev.astype(bf16), xb], axis=0)
    for f in range(D_FF // FFN_COL_TILE):
        cs = slice(f * FFN_COL_TILE, (f + 1) * FFN_COL_TILE)
        g_all = _dot(x_all, wi_ref[:, cs])
        u = _dot(xb, wi_ref[:, D_FF + f * FFN_COL_TILE:D_FF + (f + 1) * FFN_COL_TILE])
        y = (cw_ref[0:1, cs] * g_all[HALO_ROWS - 2:HALO_ROWS - 2 + tm, :]
             + cw_ref[1:2, cs] * g_all[HALO_ROWS - 1:HALO_ROWS - 1 + tm, :]
             + cw_ref[2:3, cs] * g_all[HALO_ROWS:, :] + cb_ref[:, cs])
        inner = y * (GELU_C + (0.044715 * GELU_C) * (y * y))
        h_ref[:, cs] = ((0.5 * y) * (1.0 + jnp.tanh(inner)) * u).astype(bf16)
    return _dot(h_ref[...], wo_ref[...])


def _post_kernel(*refs, n_act, tiles_per_seq):
    acts = refs[:n_act]
    (w_mix_ref, x_ref, lng_ref, lnb_ref, wq_ref, kv_ref, wo_ref, wi_ref, cw_ref, cb_ref, wf_ref,
     o_ref, h_ref, halo_ref) = refs[n_act:]
    tm = x_ref.shape[0]
    i = pl.program_id(0)

    @pl.when(i == 0)
    def _():
        halo_ref[...] = jnp.zeros_like(halo_ref)

    h = None
    off = 0
    for a_ref in acts:
        k = a_ref.shape[1]
        part = _dot(a_ref[...], w_mix_ref[off:off + k, :])
        h = part if h is None else h + part
        off += k
    x = _deepnorm_ln(x_ref[...], h, lng_ref[0:1, :], lnb_ref[0:1, :])
    x = _deepnorm_ln(x, _mem_attention_block(x, wq_ref, kv_ref, wo_ref), lng_ref[1:2, :], lnb_ref[1:2, :])
    x_prev = jnp.where((i % tiles_per_seq) != 0, halo_ref[...], 0.0)
    halo_ref[...] = x[tm - HALO_ROWS:, :]
    h = _conv_ffn_block(x, x_prev, wi_ref, cw_ref, cb_ref, wf_ref, h_ref)
    o_ref[...] = _deepnorm_ln(x, h, lng_ref[2:3, :], lnb_ref[2:3, :])


def _post_mixer(acts, w_mix, x, ln_g, ln_b, wq, kv, wo, wi, cw, cb, wf, B, S, tm):
    T = x.shape[0]
    tiles_per_seq = S // tm
    M = kv.shape[0] // B
    resident = lambda a: pl.BlockSpec(a.shape, lambda i: (0, 0), pipeline_mode=pl.Buffered(1))
    small = lambda a: pl.BlockSpec(a.shape, lambda i: (0, 0))
    return pl.pallas_call(
        functools.partial(_post_kernel, n_act=len(acts), tiles_per_seq=tiles_per_seq),
        grid=(T // tm,),
        in_specs=[pl.BlockSpec((tm, a.shape[1]), lambda i: (i, 0)) for a in acts] + [
            resident(w_mix),
            pl.BlockSpec((tm, D_MODEL), lambda i: (i, 0)),
            small(ln_g), small(ln_b),
            resident(wq),
            pl.BlockSpec((M, 2 * D_MODEL), lambda i: (i // tiles_per_seq, 0)),
            resident(wo), resident(wi), small(cw), small(cb), resident(wf),
        ],
        out_specs=pl.BlockSpec((tm, D_MODEL), lambda i: (i, 0)),
        out_shape=jax.ShapeDtypeStruct((T, D_MODEL), f32),
        scratch_shapes=[pltpu.VMEM((tm, D_FF), bf16), pltpu.VMEM((HALO_ROWS, D_MODEL), f32)],
        compiler_params=_params(("arbitrary",)),
        name="post_mixer",
    )(*acts, w_mix, x, ln_g, ln_b, wq, kv, wo, wi, cw, cb, wf)


def _even_weight(w_in):
    qa, ka, va, ra, ga, qb, kb, vb = jnp.split(
        w_in, np.cumsum([256, 256, 512, 512, GLA_GATE_RANK, 512, 512, 512])[:-1].tolist(), axis=1)
    pad = jnp.zeros((D_MODEL, LANES - GLA_GATE_RANK), w_in.dtype)
    return jnp.concatenate(
        [qa * GLA_DK ** -0.5, ka, ra, va, qb * HEAD_DIM ** -0.5, kb, ga, pad, vb], axis=1).astype(bf16)


def _odd_weight(w_in):
    return jnp.concatenate([w_in[:, :1024] * HEAD_DIM ** -0.5, w_in[:, 1024:]], axis=1).astype(bf16)


def kernel(x, mem, positions, even_w_in, even_w_out, gla_gate_w, gla_gate_b, gla_norm_g, diff_lam_q1, diff_lam_k1, diff_lam_q2, diff_lam_k2, diff_norm_g, odd_w_in, odd_w_out, swa_sinks, mem_w_q, mem_w_kv, mem_w_out, ffn_w_in, ffn_conv_w, ffn_conv_b, ffn_w_out, ln_g, ln_b):
    B, S, D = x.shape
    T = B * S
    tm = min(512, S)
    xf = x.reshape(T, D)

    inv_freq = ROPE_THETA ** (-jnp.arange(0, HEAD_DIM, 2, dtype=f32) / HEAD_DIM)
    ang = positions.astype(f32).reshape(T, 1) * inv_freq
    cos, sin = jnp.cos(ang), jnp.sin(ang)
    cos_t = jnp.tile(cos, (1, 4))
    sin_t = jnp.concatenate([-sin, sin, -sin, sin], axis=1)

    memf = mem.reshape(B * mem.shape[1], D)
    row2 = lambda v: v.reshape(1, -1)

    for i in range(DEPTH):
        j = i // 2
        if i % 2 == 0:
            lam_init = 0.8 - 0.6 * float(np.exp(-0.3 * i))
            of, ob, vt = _even_in_proj(xf, _even_weight(even_w_in[j]), cos_t, sin_t, B, S, tm)
            gate_w = jnp.pad(gla_gate_w[j], ((0, LANES - GLA_GATE_RANK), (0, 0))).astype(bf16)
            o_a = _gla(of, ob, gate_w, row2(gla_gate_b[j]), row2(gla_norm_g[j]), B, S, min(4, S // CHUNK))
            o_b = _diff_attention(ob, vt, row2(diff_lam_q1[j]), row2(diff_lam_k1[j]), row2(diff_lam_q2[j]),
                                  row2(diff_lam_k2[j]), diff_norm_g[j].reshape(-1, 1), lam_init, B, S,
                                  min(1024, S))
            acts, w_mix = [o_a, o_b], even_w_out[j]
        else:
            qt, k, vt = _odd_in_proj(xf, _odd_weight(odd_w_in[j]), cos_t, sin_t, B, S, tm)
            sink_rows = jnp.repeat(swa_sinks[j].reshape(SWA_KV_HEADS, SWA_GROUP), SWA_PAIR, axis=1)
            acts, w_mix = [_swa(qt, k, vt, sink_rows, B, S, min(4, S // SWA_PAIR))], odd_w_out[j]

        kv = _matmul(memf, mem_w_kv[i].astype(bf16), memf.shape[0], bf16)
        xf = _post_mixer(acts, w_mix.astype(bf16), xf, ln_g[i], ln_b[i],
                         (mem_w_q[i] * MEM_DH ** -0.5).astype(bf16), kv, mem_w_out[i].astype(bf16),
                         ffn_w_in[i].astype(bf16), ffn_conv_w[i], row2(ffn_conv_b[i]), ffn_w_out[i].astype(bf16),
                         B, S, tm)
    return xf.reshape(B, S, D)
```

```python
import functools

import numpy as np
import jax
import jax.numpy as jnp
from jax import lax
from jax.experimental import pallas as pl
from jax.experimental.pallas import tpu as pltpu

f32 = jnp.float32
bf16 = jnp.bfloat16

D_MODEL = 1024
DEPTH = 4
CHUNK = 64
ROPE_THETA = 10000.0
LN_EPS = 1e-5
RMS_EPS = 1e-6
NEG_INF = -1e30
HEAD_DIM = 64

GLA_HEADS = 4
GLA_DK = 64
GLA_DV = 128
GLA_GATE_RANK = 16
GLA_TAU = 16.0

DIFF_HEADS = 4
DIFF_DV = 128

SWA_Q_HEADS = 16
SWA_KV_HEADS = 2
SWA_GROUP = SWA_Q_HEADS // SWA_KV_HEADS
WINDOW_CHUNKS = 2

MEM_HEADS = 4
MEM_DH = D_MODEL // MEM_HEADS

D_FF = 2816
DEEPNORM_ALPHA = (2 * DEPTH) ** 0.25
LOG2E = 1.4426950408889634

LANES = 128
VMEM_LIMIT = 56 * 1024 * 1024

_NT = (((1,), (1,)), ((), ()))
_TN = (((0,), (0,)), ((), ()))


def _params(semantics):
    return pltpu.CompilerParams(dimension_semantics=semantics, vmem_limit_bytes=VMEM_LIMIT)


def _dot(a, b):
    return jnp.dot(a, b, preferred_element_type=f32)


def _deepnorm_ln(x, h, g, b):
    y = DEEPNORM_ALPHA * x + h
    mu = jnp.mean(y, axis=-1, keepdims=True)
    d = y - mu
    var = jnp.mean(d * d, axis=-1, keepdims=True)
    return d * lax.rsqrt(var + LN_EPS) * g + b


def _rope(y, cos_t, sin_t):
    rows = y.shape[0]
    lane = lax.broadcasted_iota(jnp.int32, (rows, LANES), 1)
    first_half = (lane & 32) == 0
    outs = []
    for j in range(y.shape[1] // LANES):
        t = y[:, j * LANES:(j + 1) * LANES]
        swapped = jnp.where(first_half, pltpu.roll(t, LANES - 32, 1), pltpu.roll(t, 32, 1))
        outs.append(t * cos_t + swapped * sin_t)
    return jnp.concatenate(outs, axis=1)


EVEN_F32_COLS = 1024
EVEN_BF_COLS = 1664
DIFF_V_COLS = DIFF_HEADS * DIFF_DV


def _even_in_kernel(x_ref, w_ref, cos_ref, sin_ref, of_ref, ob_ref, vt_ref):
    xb = x_ref[...].astype(bf16)
    of_ref[...] = _dot(xb, w_ref[:, 0:1024])
    ob_ref[:, 0:512] = _dot(xb, w_ref[:, 1024:1536]).astype(bf16)
    c = cos_ref[...]
    s = sin_ref[...]
    ob_ref[:, 512:1024] = _rope(_dot(xb, w_ref[:, 1536:2048]), c * LOG2E, s * LOG2E).astype(bf16)
    ob_ref[:, 1024:1536] = _rope(_dot(xb, w_ref[:, 2048:2560]), c, s).astype(bf16)
    ob_ref[:, 1536:1664] = _dot(xb, w_ref[:, 2560:2688]).astype(bf16)
    vt_ref[...] = _dot(xb, w_ref[:, 2688:3200]).T.astype(bf16)


def _even_in_proj(x, w, cos_t, sin_t, B, S, tm):
    T = x.shape[0]
    nt = S // tm
    return pl.pallas_call(
        _even_in_kernel,
        grid=(T // tm,),
        in_specs=[
            pl.BlockSpec((tm, D_MODEL), lambda i: (i, 0)),
            pl.BlockSpec(w.shape, lambda i: (0, 0)),
            pl.BlockSpec((tm, LANES), lambda i: (i, 0)),
            pl.BlockSpec((tm, LANES), lambda i: (i, 0)),
        ],
        out_specs=[
            pl.BlockSpec((tm, EVEN_F32_COLS), lambda i: (i, 0)),
            pl.BlockSpec((tm, EVEN_BF_COLS), lambda i: (i, 0)),
            pl.BlockSpec((DIFF_V_COLS, tm), lambda i: (i // nt, i % nt)),
        ],
        out_shape=[
            jax.ShapeDtypeStruct((T, EVEN_F32_COLS), f32),
            jax.ShapeDtypeStruct((T, EVEN_BF_COLS), bf16),
            jax.ShapeDtypeStruct((B * DIFF_V_COLS, S), bf16),
        ],
        compiler_params=_params(("parallel",)),
        name="even_in_proj",
    )(x, w, cos_t, sin_t)


def _gla_kernel(of_ref, va_ref, ga_ref, gw_ref, gb_ref, ng_ref, o_ref, st_ref, *, nchunk):
    nb = of_ref.shape[0]
    tc = nchunk * CHUNK
    hk = GLA_HEADS * GLA_DK

    @pl.when(pl.program_id(0) == 0)
    def _():
        st_ref[...] = jnp.zeros_like(st_ref)

    row = lax.broadcasted_iota(jnp.int32, (tc, tc), 0)
    col = lax.broadcasted_iota(jnp.int32, (tc, tc), 1)
    causal = (col <= row) & ((col // CHUNK) == (row // CHUNK))
    tri = jnp.where(causal, 1.0, 0.0).astype(bf16)
    own_chunk = (lax.broadcasted_iota(jnp.int32, (tc, nchunk * GLA_DK), 0) // CHUNK
                 == lax.broadcasted_iota(jnp.int32, (tc, nchunk * GLA_DK), 1) // GLA_DK)
    gain = ng_ref[...]

    for bi in range(nb):
        z = _dot(ga_ref[bi], gw_ref[...]) + gb_ref[...]
        log_a = (jnp.minimum(z, 0.0) - jnp.log(1.0 + jnp.exp(-jnp.abs(z)))) * (1.0 / GLA_TAU)

        hi = log_a.astype(bf16)
        rem = log_a - hi.astype(f32)
        mid = rem.astype(bf16)
        lo = (rem - mid.astype(f32)).astype(bf16)
        b = _dot(tri, hi) + _dot(tri, mid) + _dot(tri, lo)

        b3 = b.reshape(nchunk, CHUNK, hk)
        b_last = b3[:, CHUNK - 1:CHUNK, :]
        k_carry = jnp.exp(b_last - b3).reshape(tc, hk)
        decay = jnp.exp(b_last)

        qa = of_ref[bi, :, 0:hk]
        ka = of_ref[bi, :, hk:2 * hk]
        q_dec = (qa * jnp.exp(b)).astype(bf16)
        k_neg = (ka * jnp.exp(-b)).astype(bf16)
        k_dec = (ka * k_carry).astype(bf16)

        for h in range(GLA_HEADS):
            ks = slice(h * GLA_DK, (h + 1) * GLA_DK)
            vs = slice(h * GLA_DV, (h + 1) * GLA_DV)
            qd = q_dec[:, ks]
            v = va_ref[bi, :, vs]
            a = lax.dot_general(qd, k_neg[:, ks], _NT, preferred_element_type=f32)
            o = _dot(jnp.where(causal, a, 0.0).astype(bf16), v)
            k_blocks = jnp.where(own_chunk, jnp.concatenate([k_dec[:, ks]] * nchunk, axis=1), 0.0)
            kv_t = lax.dot_general(v, k_blocks, _TN, preferred_element_type=f32)
            state_t = st_ref[bi, h]
            states = []
            for c in range(nchunk):
                states.append(state_t.astype(bf16))
                state_t = decay[c, :, ks] * state_t + kv_t[:, c * GLA_DK:(c + 1) * GLA_DK]
            st_ref[bi, h] = state_t
            read = lax.dot_general(qd, jnp.concatenate(states, axis=0), _NT,
                                   preferred_element_type=f32)
            o = o + jnp.concatenate([read[c * CHUNK:(c + 1) * CHUNK, c * GLA_DV:(c + 1) * GLA_DV]
                                     for c in range(nchunk)], axis=0)
            o = o * lax.rsqrt(jnp.mean(o * o, axis=-1, keepdims=True) + RMS_EPS) * gain
            rr = of_ref[bi, :, 2 * hk + h * GLA_DV:2 * hk + (h + 1) * GLA_DV]
            o_ref[bi, :, vs] = (o * (rr * (1.0 / (1.0 + jnp.exp(-rr))))).astype(bf16)


def _gla(of, ob, gate_w, gate_b, norm_g, B, S, nchunk):
    T = of.shape[0]
    tc = nchunk * CHUNK
    full = lambda a: pl.BlockSpec(a.shape, lambda i: (0, 0))
    out = pl.pallas_call(
        functools.partial(_gla_kernel, nchunk=nchunk),
        grid=(S // tc,),
        in_specs=[
            pl.BlockSpec((B, tc, EVEN_F32_COLS), lambda i: (0, i, 0)),
            pl.BlockSpec((B, tc, 512), lambda i: (0, i, 0)),
            pl.BlockSpec((B, tc, LANES), lambda i: (0, i, 1536 // LANES)),
            full(gate_w), full(gate_b), full(norm_g),
        ],
        out_specs=pl.BlockSpec((B, tc, GLA_HEADS * GLA_DV), lambda i: (0, i, 0)),
        out_shape=jax.ShapeDtypeStruct((B, S, GLA_HEADS * GLA_DV), bf16),
        scratch_shapes=[pltpu.VMEM((B, GLA_HEADS, GLA_DV, GLA_DK), f32)],
        compiler_params=_params(("arbitrary",)),
        name="gla",
    )(of.reshape(B, S, -1), ob.reshape(B, S, -1), ob.reshape(B, S, -1), gate_w, gate_b, norm_g)
    return out.reshape(T, -1)


def _diff_kernel(q_ref, k_ref, vt_ref, lq1_ref, lk1_ref, lq2_ref, lk2_ref, ng_ref, o_ref, s_ref, *, tq, tk,
                 lam_init):
    i = pl.program_id(2)
    q = q_ref[...]
    q_maps = (q[:, 0:HEAD_DIM], q[:, HEAD_DIM:2 * HEAD_DIM])

    def produce(j, slot, first_query=0):
        start = pl.multiple_of(j * tk, tk)
        k = k_ref[pl.ds(start, tk), :]
        col_max = []
        for t in range(2):
            s = lax.dot_general(k[:, t * HEAD_DIM:(t + 1) * HEAD_DIM], q_maps[t][first_query:, :], _NT,
                                preferred_element_type=f32)
            s_ref[slot, t, :, first_query:] = s
            col_max.append(jnp.max(s, axis=0, keepdims=True))
        return tuple(col_max)

    def softmax_step(s, vt, tile_max, m, l, acc):
        m_new = jnp.maximum(m, tile_max)
        alpha = jnp.exp2(m - m_new)
        p = jnp.exp2(s - m_new)
        l = alpha * l + jnp.sum(p, axis=0, keepdims=True)
        acc = alpha * acc + _dot(vt, p.astype(bf16))
        return [m_new, l, acc]

    def consume(j, slot, col_max, state, kind):
        start = pl.multiple_of(j * tk, tk)
        vt = vt_ref[:, pl.ds(start, tk)]
        if kind != "visible":
            diag = (lax.broadcasted_iota(jnp.int32, (tk, 1), 0) // CHUNK
                    <= lax.broadcasted_iota(jnp.int32, (1, tk), 1) // CHUNK)
        out = []
        for t in range(2):
            m, l, acc = state[3 * t:3 * t + 3]
            if kind == "visible":
                out += softmax_step(s_ref[slot, t], vt, col_max[t], m, l, acc)
            elif kind == "diag_first":
                s = s_ref[slot, t]
                s = jnp.concatenate([jnp.where(diag, s[:, :tk], NEG_INF), s[:, tk:]], axis=1)
                out += softmax_step(s, vt, jnp.max(s, axis=0, keepdims=True), m, l, acc)
            else:
                s = jnp.where(diag, s_ref[slot, t, :, tk:], NEG_INF)
                upd = softmax_step(s, vt, jnp.max(s, axis=0, keepdims=True), m[:, tk:], l[:, tk:], acc[:, tk:])
                out += [jnp.concatenate([old[:, :tk], new], axis=1) for old, new in zip((m, l, acc), upd)]
        return tuple(out)

    def pair(jj, carry):
        j = 2 * jj
        max1 = produce(j + 1, 1)
        state = consume(j, 0, carry[6:], carry[:6], "visible")
        max0 = produce(j + 2, 0)
        state = consume(j + 1, 1, max1, state, "visible")
        return state + max0

    init = (jnp.full((1, tq), NEG_INF, f32), jnp.zeros((1, tq), f32), jnp.zeros((DIFF_DV, tq), f32)) * 2
    carry = lax.fori_loop(0, i, pair, init + produce(0, 0))
    produce(2 * i + 1, 1, first_query=tk)
    state = consume(2 * i, 0, None, carry[:6], "diag_first")
    m0, l0, acc0, m1, l1, acc1 = consume(2 * i + 1, 1, None, state, "diag_second")

    lam = (jnp.exp(jnp.sum(lq1_ref[...] * lk1_ref[...], axis=1, keepdims=True))
           - jnp.exp(jnp.sum(lq2_ref[...] * lk2_ref[...], axis=1, keepdims=True)) + lam_init)
    o = acc0 * (1.0 / l0) - lam * (acc1 * (1.0 / l1))
    o = o * lax.rsqrt(jnp.mean(o * o, axis=0, keepdims=True) + RMS_EPS) * ng_ref[...]
    o_ref[...] = (o * (1.0 - lam_init)).T.astype(bf16)


def _diff_attention(ob, vt, lq1, lk1, lq2, lk2, norm_g, lam_init, B, S, tq):
    T = ob.shape[0]
    nq = S // tq
    tk = tq // 2
    small = lambda a: pl.BlockSpec(a.shape, lambda b, h, i: (0, 0))
    return pl.pallas_call(
        functools.partial(_diff_kernel, tq=tq, tk=tk, lam_init=lam_init),
        grid=(B, DIFF_HEADS, nq),
        in_specs=[
            pl.BlockSpec((tq, LANES), lambda b, h, i: (b * nq + i, 4 + h)),
            pl.BlockSpec((S, LANES), lambda b, h, i: (b, 8 + h)),
            pl.BlockSpec((DIFF_DV, S), lambda b, h, i: (b * DIFF_HEADS + h, 0)),
            small(lq1), small(lk1), small(lq2), small(lk2), small(norm_g),
        ],
        out_specs=pl.BlockSpec((tq, DIFF_DV), lambda b, h, i: (b * nq + i, h)),
        out_shape=jax.ShapeDtypeStruct((T, DIFF_HEADS * DIFF_DV), bf16),
        scratch_shapes=[pltpu.VMEM((2, 2, tk, tq), f32)],
        compiler_params=_params(("parallel", "parallel", "arbitrary")),
        name="diff_attention",
    )(ob, ob, vt, lq1, lk1, lq2, lk2, norm_g)


SWA_Q_COLS = SWA_Q_HEADS * HEAD_DIM
SWA_KV_COLS = SWA_KV_HEADS * HEAD_DIM
SWA_PAIR = 2 * CHUNK
SWA_BAND = (WINDOW_CHUNKS + 2) * CHUNK


def _odd_in_kernel(x_ref, wqt_ref, wk_ref, wvt_ref, cos_ref, sin_ref, cost_ref, sint_ref, qt_ref, k_ref, vt_ref):
    x = x_ref[...]
    xb = x.astype(bf16)
    xt = x.T.astype(bf16)
    qt = _dot(wqt_ref[...], xt)
    c = cost_ref[...] * LOG2E
    s = sint_ref[...] * LOG2E
    half = HEAD_DIM // 2
    rows = []
    for h in range(SWA_Q_HEADS):
        t1 = qt[h * HEAD_DIM:h * HEAD_DIM + half, :]
        t2 = qt[h * HEAD_DIM + half:(h + 1) * HEAD_DIM, :]
        rows += [t1 * c - t2 * s, t2 * c + t1 * s]
    qt_ref[...] = jnp.concatenate(rows, axis=0).astype(bf16)
    vt_ref[...] = _dot(wvt_ref[...], xt).astype(bf16)
    k_ref[...] = _rope(_dot(xb, wk_ref[...]), cos_ref[...], sin_ref[...]).astype(bf16)


def _odd_in_proj(x, wqt, wk, wvt, cos_t, sin_t, cos_ft, sin_ft, B, S, tm):
    T = x.shape[0]
    nt = S // tm
    full = lambda a: pl.BlockSpec(a.shape, lambda i: (0, 0))
    return pl.pallas_call(
        _odd_in_kernel,
        grid=(T // tm,),
        in_specs=[
            pl.BlockSpec((tm, D_MODEL), lambda i: (i, 0)),
            full(wqt), full(wk), full(wvt),
            pl.BlockSpec((tm, LANES), lambda i: (i, 0)),
            pl.BlockSpec((tm, LANES), lambda i: (i, 0)),
            pl.BlockSpec((HEAD_DIM // 2, tm), lambda i: (0, i)),
            pl.BlockSpec((HEAD_DIM // 2, tm), lambda i: (0, i)),
        ],
        out_specs=[
            pl.BlockSpec((SWA_Q_COLS, tm), lambda i: (i // nt, i % nt)),
            pl.BlockSpec((tm, SWA_KV_COLS), lambda i: (i, 0)),
            pl.BlockSpec((SWA_KV_COLS, tm), lambda i: (i // nt, i % nt)),
        ],
        out_shape=[
            jax.ShapeDtypeStruct((B * SWA_Q_COLS, S), bf16),
            jax.ShapeDtypeStruct((T, SWA_KV_COLS), bf16),
            jax.ShapeDtypeStruct((B * SWA_KV_COLS, S), bf16),
        ],
        compiler_params=_params(("parallel",)),
        name="odd_in_proj",
    )(x, wqt, wk, wvt, cos_t, sin_t, cos_ft, sin_ft)


def _swa_kernel(qt_ref, k_ref, vt_ref, sink_ref, o_ref, *, npair):
    i = pl.program_id(1)
    ncol = SWA_GROUP * SWA_PAIR
    key_chunk = lax.broadcasted_iota(jnp.int32, (SWA_BAND, ncol), 0) // CHUNK
    qry_chunk = (lax.broadcasted_iota(jnp.int32, (SWA_BAND, ncol), 1) // CHUNK) % 2
    bias_inner = jnp.where((key_chunk >= qry_chunk) & (key_chunk <= qry_chunk + WINDOW_CHUNKS), 0.0, NEG_INF)
    bias_first = jnp.where(key_chunk <= qry_chunk, 0.0, NEG_INF)
    for c in range(npair):
        pair = i * npair + c
        start = pl.multiple_of(jnp.maximum(pair - 1, 0) * SWA_PAIR, SWA_PAIR)
        kb = k_ref[pl.ds(start, SWA_BAND), :]
        vtb = vt_ref[:, pl.ds(start, SWA_BAND)]
        bias = jnp.where(i == 0, bias_first, bias_inner) if c == 0 else bias_inner
        rs = slice(c * SWA_PAIR, (c + 1) * SWA_PAIR)
        for j in range(SWA_KV_HEADS):
            hs = slice(j * HEAD_DIM, (j + 1) * HEAD_DIM)
            qt = jnp.concatenate(
                [qt_ref[(j * SWA_GROUP + g) * HEAD_DIM:(j * SWA_GROUP + g + 1) * HEAD_DIM, rs]
                 for g in range(SWA_GROUP)], axis=1)
            s = _dot(kb[:, hs], qt) + bias
            sink = sink_ref[j:j + 1, :] * LOG2E
            m = jnp.maximum(jnp.max(s, axis=0, keepdims=True), sink)
            p = jnp.exp2(s - m)
            den = jnp.sum(p, axis=0, keepdims=True) + jnp.exp2(sink - m)
            o = _dot(vtb[hs, :], p.astype(bf16)) * (1.0 / den)
            for gp in range(SWA_GROUP // 2):
                two_heads = jnp.concatenate([o[:, (2 * gp) * SWA_PAIR:(2 * gp + 1) * SWA_PAIR],
                                             o[:, (2 * gp + 1) * SWA_PAIR:(2 * gp + 2) * SWA_PAIR]], axis=0)
                col = (j * SWA_GROUP + 2 * gp) * HEAD_DIM
                o_ref[rs, col:col + 2 * HEAD_DIM] = two_heads.T.astype(bf16)


def _swa(qt, k, vt, sink_rows, B, S, npair):
    T = k.shape[0]
    tq = npair * SWA_PAIR
    nt = S // tq
    return pl.pallas_call(
        functools.partial(_swa_kernel, npair=npair),
        grid=(B, nt),
        in_specs=[
            pl.BlockSpec((SWA_Q_COLS, tq), lambda b, i: (b, i)),
            pl.BlockSpec((S, SWA_KV_COLS), lambda b, i: (b, 0)),
            pl.BlockSpec((SWA_KV_COLS, S), lambda b, i: (b, 0)),
            pl.BlockSpec(sink_rows.shape, lambda b, i: (0, 0)),
        ],
        out_specs=pl.BlockSpec((tq, SWA_Q_COLS), lambda b, i: (b * nt + i, 0)),
        out_shape=jax.ShapeDtypeStruct((T, SWA_Q_COLS), bf16),
        compiler_params=_params(("parallel", "arbitrary")),
        name="swa",
    )(qt, k, vt, sink_rows)


def _matmul_kernel(a_ref, w_ref, o_ref):
    o_ref[...] = _dot(a_ref[...].astype(bf16), w_ref[...]).astype(o_ref.dtype)


def _matmul(a, w, tm, out_dtype):
    M, K = a.shape
    N = w.shape[1]
    return pl.pallas_call(
        _matmul_kernel,
        grid=(M // tm,),
        in_specs=[pl.BlockSpec((tm, K), lambda i: (i, 0)), pl.BlockSpec((K, N), lambda i: (0, 0))],
        out_specs=pl.BlockSpec((tm, N), lambda i: (i, 0)),
        out_shape=jax.ShapeDtypeStruct((M, N), out_dtype),
        compiler_params=_params(("parallel",)),
        name="matmul",
    )(a, w)


FFN_COL_TILE = 256
HALO_ROWS = 16
GELU_C = float(np.sqrt(2.0 / np.pi))


def _mem_attention_block(x, wq_ref, kv_ref, wo_ref):
    q = _dot(x.astype(bf16), wq_ref[...]).astype(bf16)
    heads = []
    for h in range(MEM_HEADS):
        hs = slice(h * MEM_DH, (h + 1) * MEM_DH)
        s = lax.dot_general(q[:, hs], kv_ref[:, hs], _NT, preferred_element_type=f32)
        p = jnp.exp(s - jnp.max(s, axis=1, keepdims=True))
        inv = 1.0 / jnp.sum(p, axis=1, keepdims=True)
        o = _dot(p.astype(bf16), kv_ref[:, D_MODEL + h * MEM_DH:D_MODEL + (h + 1) * MEM_DH]) * inv
        heads.append(o.astype(bf16))
    return _dot(jnp.concatenate(heads, axis=1), wo_ref[...])


def _conv_ffn_block(x, x_prev, wi_ref, cw_ref, cb_ref, wo_ref, h_ref):
    tm = x.shape[0]
    xb = x.astype(bf16)
    x_all = jnp.concatenate([x_prev.astype(bf16), xb], axis=0)
    for f in range(D_FF // FFN_COL_TILE):
        cs = slice(f * FFN_COL_TILE, (f + 1) * FFN_COL_TILE)
        g_all = _dot(x_all, wi_ref[:, cs])
        u = _dot(xb, wi_ref[:, D_FF + f * FFN_COL_TILE:D_FF + (f + 1) * FFN_COL_TILE])
        y = (cw_ref[0:1, cs] * g_all[HALO_ROWS - 2:HALO_ROWS - 2 + tm, :]
             + cw_ref[1:2, cs] * g_all[HALO_ROWS - 1:HALO_ROWS - 1 + tm, :]
             + cw_ref[2:3, cs] * g_all[HALO_ROWS:, :] + cb_ref[:, cs])
        inner = y * (GELU_C + (0.044715 * GELU_C) * (y * y))
        h_ref[:, cs] = ((0.5 * y) * (1.0 + jnp.tanh(inner)) * u).astype(bf16)
    return _dot(h_ref[...], wo_ref[...])


def _post_kernel(*refs, n_act, tiles_per_seq):
    acts = refs[:n_act]
    (w_mix_ref, x_ref, lng_ref, lnb_ref, wq_ref, kv_ref, wo_ref, wi_ref, cw_ref, cb_ref, wf_ref,
     o_ref, h_ref, halo_ref) = refs[n_act:]
    tm = x_ref.shape[0]
    i = pl.program_id(0)

    @pl.when(i == 0)
    def _():
        halo_ref[...] = jnp.zeros_like(halo_ref)

    h = None
    off = 0
    for a_ref in acts:
        k = a_ref.shape[1]
        part = _dot(a_ref[...], w_mix_ref[off:off + k, :])
        h = part if h is None else h + part
        off += k
    x = _deepnorm_ln(x_ref[...], h, lng_ref[0:1, :], lnb_ref[0:1, :])
    x = _deepnorm_ln(x, _mem_attention_block(x, wq_ref, kv_ref, wo_ref), lng_ref[1:2, :], lnb_ref[1:2, :])
    x_prev = jnp.where((i % tiles_per_seq) != 0, halo_ref[...], 0.0)
    halo_ref[...] = x[tm - HALO_ROWS:, :]
    h = _conv_ffn_block(x, x_prev, wi_ref, cw_ref, cb_ref, wf_ref, h_ref)
    o_ref[...] = _deepnorm_ln(x, h, lng_ref[2:3, :], lnb_ref[2:3, :])


def _post_mixer(acts, w_mix, x, ln_g, ln_b, wq, kv, wo, wi, cw, cb, wf, B, S, tm):
    T = x.shape[0]
    tiles_per_seq = S // tm
    M = kv.shape[0] // B
    resident = lambda a: pl.BlockSpec(a.shape, lambda i: (0, 0), pipeline_mode=pl.Buffered(1))
    small = lambda a: pl.BlockSpec(a.shape, lambda i: (0, 0))
    return pl.pallas_call(
        functools.partial(_post_kernel, n_act=len(acts), tiles_per_seq=tiles_per_seq),
        grid=(T // tm,),
        in_specs=[pl.BlockSpec((tm, a.shape[1]), lambda i: (i, 0)) for a in acts] + [
            resident(w_mix),
            pl.BlockSpec((tm, D_MODEL), lambda i: (i, 0)),
            small(ln_g), small(ln_b),
            resident(wq),
            pl.BlockSpec((M, 2 * D_MODEL), lambda i: (i // tiles_per_seq, 0)),
            resident(wo), resident(wi), small(cw), small(cb), resident(wf),
        ],
        out_specs=pl.BlockSpec((tm, D_MODEL), lambda i: (i, 0)),
        out_shape=jax.ShapeDtypeStruct((T, D_MODEL), f32),
        scratch_shapes=[pltpu.VMEM((tm, D_FF), bf16), pltpu.VMEM((HALO_ROWS, D_MODEL), f32)],
        compiler_params=_params(("arbitrary",)),
        name="post_mixer",
    )(*acts, w_mix, x, ln_g, ln_b, wq, kv, wo, wi, cw, cb, wf)


def _even_weight(w_in):
    qa, ka, va, ra, ga, qb, kb, vb = jnp.split(
        w_in, np.cumsum([256, 256, 512, 512, GLA_GATE_RANK, 512, 512, 512])[:-1].tolist(), axis=1)
    pad = jnp.zeros((D_MODEL, LANES - GLA_GATE_RANK), w_in.dtype)
    return jnp.concatenate(
        [qa * GLA_DK ** -0.5, ka, ra, va, qb * HEAD_DIM ** -0.5, kb, ga, pad, vb], axis=1).astype(bf16)


def _odd_weights(w_in):
    wq, wk, wv = w_in[:, :SWA_Q_COLS], w_in[:, SWA_Q_COLS:SWA_Q_COLS + SWA_KV_COLS], w_in[:, SWA_Q_COLS + SWA_KV_COLS:]
    return (wq * HEAD_DIM ** -0.5).T.astype(bf16), wk.astype(bf16), wv.T.astype(bf16)


def kernel(x, mem, positions, even_w_in, even_w_out, gla_gate_w, gla_gate_b, gla_norm_g, diff_lam_q1, diff_lam_k1, diff_lam_q2, diff_lam_k2, diff_norm_g, odd_w_in, odd_w_out, swa_sinks, mem_w_q, mem_w_kv, mem_w_out, ffn_w_in, ffn_conv_w, ffn_conv_b, ffn_w_out, ln_g, ln_b):
    B, S, D = x.shape
    T = B * S
    tm = min(512, S)
    xf = x.reshape(T, D)

    inv_freq = ROPE_THETA ** (-jnp.arange(0, HEAD_DIM, 2, dtype=f32) / HEAD_DIM)
    ang = positions.astype(f32).reshape(T, 1) * inv_freq
    cos, sin = jnp.cos(ang), jnp.sin(ang)
    cos_t = jnp.tile(cos, (1, 4))
    sin_t = jnp.concatenate([-sin, sin, -sin, sin], axis=1)

    memf = mem.reshape(B * mem.shape[1], D)
    row2 = lambda v: v.reshape(1, -1)

    for i in range(DEPTH):
        j = i // 2
        if i % 2 == 0:
            lam_init = 0.8 - 0.6 * float(np.exp(-0.3 * i))
            of, ob, vt = _even_in_proj(xf, _even_weight(even_w_in[j]), cos_t, sin_t, B, S, tm)
            gate_w = jnp.pad(gla_gate_w[j], ((0, LANES - GLA_GATE_RANK), (0, 0))).astype(bf16)
            o_a = _gla(of, ob, gate_w, row2(gla_gate_b[j]), row2(gla_norm_g[j]), B, S, min(4, S // CHUNK))
            o_b = _diff_attention(ob, vt, row2(diff_lam_q1[j]), row2(diff_lam_k1[j]), row2(diff_lam_q2[j]),
                                  row2(diff_lam_k2[j]), diff_norm_g[j].reshape(-1, 1), lam_init, B, S,
                                  min(1024, S))
            acts, w_mix = [o_a, o_b], even_w_out[j]
        else:
            qt, k, vt = _odd_in_proj(xf, *_odd_weights(odd_w_in[j]), cos_t, sin_t, cos.T, sin.T, B, S, tm)
            sink_rows = jnp.repeat(swa_sinks[j].reshape(SWA_KV_HEADS, SWA_GROUP), SWA_PAIR, axis=1)
            acts, w_mix = [_swa(qt, k, vt, sink_rows, B, S, min(4, S // SWA_PAIR))], odd_w_out[j]

        kv = _matmul(memf, mem_w_kv[i].astype(bf16), memf.shape[0], bf16)
        xf = _post_mixer(acts, w_mix.astype(bf16), xf, ln_g[i], ln_b[i],
                         (mem_w_q[i] * MEM_DH ** -0.5).astype(bf16), kv, mem_w_out[i].astype(bf16),
                         ffn_w_in[i].astype(bf16), ffn_conv_w[i], row2(ffn_conv_b[i]), ffn_w_out[i].astype(bf16),
                         B, S, tm)
    return xf.reshape(B, S, D)
```

```python
import functools

import numpy as np
import jax
import jax.numpy as jnp
from jax import lax
from jax.experimental import pallas as pl
from jax.experimental.pallas import tpu as pltpu

f32 = jnp.float32
bf16 = jnp.bfloat16

D_MODEL = 1024
DEPTH = 4
CHUNK = 64
ROPE_THETA = 10000.0
LN_EPS = 1e-5
RMS_EPS = 1e-6
NEG_INF = -1e30
HEAD_DIM = 64

GLA_HEADS = 4
GLA_DK = 64
GLA_DV = 128
GLA_GATE_RANK = 16
GLA_TAU = 16.0

DIFF_HEADS = 4
DIFF_DV = 128

SWA_Q_HEADS = 16
SWA_KV_HEADS = 2
SWA_GROUP = SWA_Q_HEADS // SWA_KV_HEADS
WINDOW_CHUNKS = 2

MEM_HEADS = 4
MEM_DH = D_MODEL // MEM_HEADS

D_FF = 2816
DEEPNORM_ALPHA = (2 * DEPTH) ** 0.25
LOG2E = 1.4426950408889634

LANES = 128
VMEM_LIMIT = 56 * 1024 * 1024

_NT = (((1,), (1,)), ((), ()))
_TN = (((0,), (0,)), ((), ()))


def _params(semantics):
    return pltpu.CompilerParams(dimension_semantics=semantics, vmem_limit_bytes=VMEM_LIMIT)


def _dot(a, b):
    return jnp.dot(a, b, preferred_element_type=f32)


def _deepnorm_ln(x, h, g, b):
    y = DEEPNORM_ALPHA * x + h
    mu = jnp.mean(y, axis=-1, keepdims=True)
    d = y - mu
    var = jnp.mean(d * d, axis=-1, keepdims=True)
    return d * lax.rsqrt(var + LN_EPS) * g + b


def _rope(y, cos_t, sin_t):
    rows = y.shape[0]
    lane = lax.broadcasted_iota(jnp.int32, (rows, LANES), 1)
    first_half = (lane & 32) == 0
    outs = []
    for j in range(y.shape[1] // LANES):
        t = y[:, j * LANES:(j + 1) * LANES]
        swapped = jnp.where(first_half, pltpu.roll(t, LANES - 32, 1), pltpu.roll(t, 32, 1))
        outs.append(t * cos_t + swapped * sin_t)
    return jnp.concatenate(outs, axis=1)


EVEN_F32_COLS = 1024
EVEN_BF_COLS = 1664
DIFF_V_COLS = DIFF_HEADS * DIFF_DV


def _even_in_kernel(x_ref, w_ref, cos_ref, sin_ref, of_ref, ob_ref, vt_ref):
    xb = x_ref[...].astype(bf16)
    of_ref[...] = _dot(xb, w_ref[:, 0:1024])
    ob_ref[:, 0:512] = _dot(xb, w_ref[:, 1024:1536]).astype(bf16)
    c = cos_ref[...]
    s = sin_ref[...]
    ob_ref[:, 512:1024] = _rope(_dot(xb, w_ref[:, 1536:2048]), c * LOG2E, s * LOG2E).astype(bf16)
    ob_ref[:, 1024:1536] = _rope(_dot(xb, w_ref[:, 2048:2560]), c, s).astype(bf16)
    ob_ref[:, 1536:1664] = _dot(xb, w_ref[:, 2560:2688]).astype(bf16)
    vt_ref[...] = _dot(xb, w_ref[:, 2688:3200]).T.astype(bf16)


def _even_in_proj(x, w, cos_t, sin_t, B, S, tm):
    T = x.shape[0]
    nt = S // tm
    return pl.pallas_call(
        _even_in_kernel,
        grid=(T // tm,),
        in_specs=[
            pl.BlockSpec((tm, D_MODEL), lambda i: (i, 0)),
            pl.BlockSpec(w.shape, lambda i: (0, 0)),
            pl.BlockSpec((tm, LANES), lambda i: (i, 0)),
            pl.BlockSpec((tm, LANES), lambda i: (i, 0)),
        ],
        out_specs=[
            pl.BlockSpec((tm, EVEN_F32_COLS), lambda i: (i, 0)),
            pl.BlockSpec((tm, EVEN_BF_COLS), lambda i: (i, 0)),
            pl.BlockSpec((DIFF_V_COLS, tm), lambda i: (i // nt, i % nt)),
        ],
        out_shape=[
            jax.ShapeDtypeStruct((T, EVEN_F32_COLS), f32),
            jax.ShapeDtypeStruct((T, EVEN_BF_COLS), bf16),
            jax.ShapeDtypeStruct((B * DIFF_V_COLS, S), bf16),
        ],
        compiler_params=_params(("parallel",)),
        name="even_in_proj",
    )(x, w, cos_t, sin_t)


def _gla_kernel(of_ref, va_ref, ga_ref, gw_ref, gb_ref, ng_ref, o_ref, st_ref, *, nchunk):
    nb = of_ref.shape[0]
    tc = nchunk * CHUNK
    hk = GLA_HEADS * GLA_DK

    @pl.when(pl.program_id(0) == 0)
    def _():
        st_ref[...] = jnp.zeros_like(st_ref)

    row = lax.broadcasted_iota(jnp.int32, (tc, tc), 0)
    col = lax.broadcasted_iota(jnp.int32, (tc, tc), 1)
    causal = (col <= row) & ((col // CHUNK) == (row // CHUNK))
    tri = jnp.where(causal, 1.0, 0.0).astype(bf16)
    own_chunk = (lax.broadcasted_iota(jnp.int32, (tc, nchunk * GLA_DK), 0) // CHUNK
                 == lax.broadcasted_iota(jnp.int32, (tc, nchunk * GLA_DK), 1) // GLA_DK)
    gain = ng_ref[...]

    for bi in range(nb):
        z = _dot(ga_ref[bi], gw_ref[...]) + gb_ref[...]
        log_a = (jnp.minimum(z, 0.0) - jnp.log(1.0 + jnp.exp(-jnp.abs(z)))) * (1.0 / GLA_TAU)

        hi = log_a.astype(bf16)
        rem = log_a - hi.astype(f32)
        mid = rem.astype(bf16)
        lo = (rem - mid.astype(f32)).astype(bf16)
        b = _dot(tri, hi) + _dot(tri, mid) + _dot(tri, lo)

        b3 = b.reshape(nchunk, CHUNK, hk)
        b_last = b3[:, CHUNK - 1:CHUNK, :]
        k_carry = jnp.exp(b_last - b3).reshape(tc, hk)
        decay = jnp.exp(b_last)

        qa = of_ref[bi, :, 0:hk]
        ka = of_ref[bi, :, hk:2 * hk]
        q_dec = (qa * jnp.exp(b)).astype(bf16)
        k_neg = (ka * jnp.exp(-b)).astype(bf16)
        k_dec = (ka * k_carry).astype(bf16)

        for h in range(GLA_HEADS):
            ks = slice(h * GLA_DK, (h + 1) * GLA_DK)
            vs = slice(h * GLA_DV, (h + 1) * GLA_DV)
            qd = q_dec[:, ks]
            v = va_ref[bi, :, vs]
            a = lax.dot_general(qd, k_neg[:, ks], _NT, preferred_element_type=f32)
            o = _dot(jnp.where(causal, a, 0.0).astype(bf16), v)
            k_blocks = jnp.where(own_chunk, jnp.concatenate([k_dec[:, ks]] * nchunk, axis=1), 0.0)
            kv_t = lax.dot_general(v, k_blocks, _TN, preferred_element_type=f32)
            state_t = st_ref[bi, h]
            states = []
            for c in range(nchunk):
                states.append(state_t.astype(bf16))
                state_t = decay[c, :, ks] * state_t + kv_t[:, c * GLA_DK:(c + 1) * GLA_DK]
            st_ref[bi, h] = state_t
            read = lax.dot_general(qd, jnp.concatenate(states, axis=0), _NT,
                                   preferred_element_type=f32)
            o = o + jnp.concatenate([read[c * CHUNK:(c + 1) * CHUNK, c * GLA_DV:(c + 1) * GLA_DV]
                                     for c in range(nchunk)], axis=0)
            o = o * lax.rsqrt(jnp.mean(o * o, axis=-1, keepdims=True) + RMS_EPS) * gain
            rr = of_ref[bi, :, 2 * hk + h * GLA_DV:2 * hk + (h + 1) * GLA_DV]
            o_ref[bi, :, vs] = (o * (rr * (1.0 / (1.0 + jnp.exp(-rr))))).astype(bf16)


def _gla(of, ob, gate_w, gate_b, norm_g, B, S, nchunk):
    T = of.shape[0]
    tc = nchunk * CHUNK
    full = lambda a: pl.BlockSpec(a.shape, lambda i: (0, 0))
    out = pl.pallas_call(
        functools.partial(_gla_kernel, nchunk=nchunk),
        grid=(S // tc,),
        in_specs=[
            pl.BlockSpec((B, tc, EVEN_F32_COLS), lambda i: (0, i, 0)),
            pl.BlockSpec((B, tc, 512), lambda i: (0, i, 0)),
            pl.BlockSpec((B, tc, LANES), lambda i: (0, i, 1536 // LANES)),
            full(gate_w), full(gate_b), full(norm_g),
        ],
        out_specs=pl.BlockSpec((B, tc, GLA_HEADS * GLA_DV), lambda i: (0, i, 0)),
        out_shape=jax.ShapeDtypeStruct((B, S, GLA_HEADS * GLA_DV), bf16),
        scratch_shapes=[pltpu.VMEM((B, GLA_HEADS, GLA_DV, GLA_DK), f32)],
        compiler_params=_params(("arbitrary",)),
        name="gla",
    )(of.reshape(B, S, -1), ob.reshape(B, S, -1), ob.reshape(B, S, -1), gate_w, gate_b, norm_g)
    return out.reshape(T, -1)


def _diff_kernel(q_ref, k_ref, vt_ref, lq1_ref, lk1_ref, lq2_ref, lk2_ref, ng_ref, o_ref, s_ref, *, tq, tk,
                 lam_init):
    i = pl.program_id(2)
    q = q_ref[...]
    q_maps = (q[:, 0:HEAD_DIM], q[:, HEAD_DIM:2 * HEAD_DIM])

    def produce(j, slot, first_query=0):
        start = pl.multiple_of(j * tk, tk)
        k = k_ref[pl.ds(start, tk), :]
        col_max = []
        for t in range(2):
            s = lax.dot_general(k[:, t * HEAD_DIM:(t + 1) * HEAD_DIM], q_maps[t][first_query:, :], _NT,
                                preferred_element_type=f32)
            s_ref[slot, t, :, first_query:] = s
            col_max.append(jnp.max(s, axis=0, keepdims=True))
        return tuple(col_max)

    def softmax_step(s, vt, tile_max, m, l, acc):
        m_new = jnp.maximum(m, tile_max)
        alpha = jnp.exp2(m - m_new)
        p = jnp.exp2(s - m_new)
        l = alpha * l + jnp.sum(p, axis=0, keepdims=True)
        acc = alpha * acc + _dot(vt, p.astype(bf16))
        return [m_new, l, acc]

    def consume(j, slot, col_max, state, kind):
        start = pl.multiple_of(j * tk, tk)
        vt = vt_ref[:, pl.ds(start, tk)]
        if kind != "visible":
            diag = (lax.broadcasted_iota(jnp.int32, (tk, 1), 0) // CHUNK
                    <= lax.broadcasted_iota(jnp.int32, (1, tk), 1) // CHUNK)
        out = []
        for t in range(2):
            m, l, acc = state[3 * t:3 * t + 3]
            if kind == "visible":
                out += softmax_step(s_ref[slot, t], vt, col_max[t], m, l, acc)
            elif kind == "diag_first":
                s = s_ref[slot, t]
                s = jnp.concatenate([jnp.where(diag, s[:, :tk], NEG_INF), s[:, tk:]], axis=1)
                out += softmax_step(s, vt, jnp.max(s, axis=0, keepdims=True), m, l, acc)
            else:
                s = jnp.where(diag, s_ref[slot, t, :, tk:], NEG_INF)
                upd = softmax_step(s, vt, jnp.max(s, axis=0, keepdims=True), m[:, tk:], l[:, tk:], acc[:, tk:])
                out += [jnp.concatenate([old[:, :tk], new], axis=1) for old, new in zip((m, l, acc), upd)]
        return tuple(out)

    def pair(jj, carry):
        j = 2 * jj
        max1 = produce(j + 1, 1)
        state = consume(j, 0, carry[6:], carry[:6], "visible")
        max0 = produce(j + 2, 0)
        state = consume(j + 1, 1, max1, state, "visible")
        return state + max0

    init = (jnp.full((1, tq), NEG_INF, f32), jnp.zeros((1, tq), f32), jnp.zeros((DIFF_DV, tq), f32)) * 2
    carry = lax.fori_loop(0, i, pair, init + produce(0, 0))
    produce(2 * i + 1, 1, first_query=tk)
    state = consume(2 * i, 0, None, carry[:6], "diag_first")
    m0, l0, acc0, m1, l1, acc1 = consume(2 * i + 1, 1, None, state, "diag_second")

    lam = (jnp.exp(jnp.sum(lq1_ref[...] * lk1_ref[...], axis=1, keepdims=True))
           - jnp.exp(jnp.sum(lq2_ref[...] * lk2_ref[...], axis=1, keepdims=True)) + lam_init)
    o = acc0 * (1.0 / l0) - lam * (acc1 * (1.0 / l1))
    o = o * lax.rsqrt(jnp.mean(o * o, axis=0, keepdims=True) + RMS_EPS) * ng_ref[...]
    o_ref[...] = (o * (1.0 - lam_init)).T.astype(bf16)


def _diff_attention(ob, vt, lq1, lk1, lq2, lk2, norm_g, lam_init, B, S, tq):
    T = ob.shape[0]
    nq = S // tq
    tk = tq // 2
    small = lambda a: pl.BlockSpec(a.shape, lambda b, h, i: (0, 0))
    return pl.pallas_call(
        functools.partial(_diff_kernel, tq=tq, tk=tk, lam_init=lam_init),
        grid=(B, DIFF_HEADS, nq),
        in_specs=[
            pl.BlockSpec((tq, LANES), lambda b, h, i: (b * nq + i, 4 + h)),
            pl.BlockSpec((S, LANES), lambda b, h, i: (b, 8 + h)),
            pl.BlockSpec((DIFF_DV, S), lambda b, h, i: (b * DIFF_HEADS + h, 0)),
            small(lq1), small(lk1), small(lq2), small(lk2), small(norm_g),
        ],
        out_specs=pl.BlockSpec((tq, DIFF_DV), lambda b, h, i: (b * nq + i, h)),
        out_shape=jax.ShapeDtypeStruct((T, DIFF_HEADS * DIFF_DV), bf16),
        scratch_shapes=[pltpu.VMEM((2, 2, tk, tq), f32)],
        compiler_params=_params(("parallel", "parallel", "arbitrary")),
        name="diff_attention",
    )(ob, ob, vt, lq1, lk1, lq2, lk2, norm_g)


SWA_Q_COLS = SWA_Q_HEADS * HEAD_DIM
SWA_KV_COLS = SWA_KV_HEADS * HEAD_DIM
SWA_PAIR = 2 * CHUNK
SWA_BAND = (WINDOW_CHUNKS + 2) * CHUNK


def _odd_in_kernel(x_ref, wqt_ref, wk_ref, wvt_ref, cos_ref, sin_ref, cost_ref, sint_ref, qt_ref, k_ref, vt_ref):
    x = x_ref[...]
    xb = x.astype(bf16)
    xt = x.T.astype(bf16)
    qt = _dot(wqt_ref[...], xt)
    c = cost_ref[...] * LOG2E
    s = sint_ref[...] * LOG2E
    half = HEAD_DIM // 2
    rows = []
    for h in range(SWA_Q_HEADS):
        t1 = qt[h * HEAD_DIM:h * HEAD_DIM + half, :]
        t2 = qt[h * HEAD_DIM + half:(h + 1) * HEAD_DIM, :]
        rows += [t1 * c - t2 * s, t2 * c + t1 * s]
    qt_ref[...] = jnp.concatenate(rows, axis=0).astype(bf16)
    vt_ref[...] = _dot(wvt_ref[...], xt).astype(bf16)
    k_ref[...] = _rope(_dot(xb, wk_ref[...]), cos_ref[...], sin_ref[...]).astype(bf16)


def _odd_in_proj(x, wqt, wk, wvt, cos_t, sin_t, cos_ft, sin_ft, B, S, tm):
    T = x.shape[0]
    nt = S // tm
    full = lambda a: pl.BlockSpec(a.shape, lambda i: (0, 0))
    return pl.pallas_call(
        _odd_in_kernel,
        grid=(T // tm,),
        in_specs=[
            pl.BlockSpec((tm, D_MODEL), lambda i: (i, 0)),
            full(wqt), full(wk), full(wvt),
            pl.BlockSpec((tm, LANES), lambda i: (i, 0)),
            pl.BlockSpec((tm, LANES), lambda i: (i, 0)),
            pl.BlockSpec((HEAD_DIM // 2, tm), lambda i: (0, i)),
            pl.BlockSpec((HEAD_DIM // 2, tm), lambda i: (0, i)),
        ],
        out_specs=[
            pl.BlockSpec((SWA_Q_COLS, tm), lambda i: (i // nt, i % nt)),
            pl.BlockSpec((tm, SWA_KV_COLS), lambda i: (i, 0)),
            pl.BlockSpec((SWA_KV_COLS, tm), lambda i: (i // nt, i % nt)),
        ],
        out_shape=[
            jax.ShapeDtypeStruct((B * SWA_Q_COLS, S), bf16),
            jax.ShapeDtypeStruct((T, SWA_KV_COLS), bf16),
            jax.ShapeDtypeStruct((B * SWA_KV_COLS, S), bf16),
        ],
        compiler_params=_params(("parallel",)),
        name="odd_in_proj",
    )(x, wqt, wk, wvt, cos_t, sin_t, cos_ft, sin_ft)


def _swa_kernel(qt_ref, k_ref, vt_ref, sink_ref, o_ref, *, npair):
    i = pl.program_id(1)
    ncol = SWA_GROUP * SWA_PAIR
    key_chunk = lax.broadcasted_iota(jnp.int32, (SWA_BAND, ncol), 0) // CHUNK
    qry_chunk = (lax.broadcasted_iota(jnp.int32, (SWA_BAND, ncol), 1) // CHUNK) % 2
    bias_inner = jnp.where((key_chunk >= qry_chunk) & (key_chunk <= qry_chunk + WINDOW_CHUNKS), 0.0, NEG_INF)
    bias_first = jnp.where(key_chunk <= qry_chunk, 0.0, NEG_INF)
    for c in range(npair):
        pair = i * npair + c
        start = pl.multiple_of(jnp.maximum(pair - 1, 0) * SWA_PAIR, SWA_PAIR)
        kb = k_ref[pl.ds(start, SWA_BAND), :]
        vtb = vt_ref[:, pl.ds(start, SWA_BAND)]
        bias = jnp.where(i == 0, bias_first, bias_inner) if c == 0 else bias_inner
        rs = slice(c * SWA_PAIR, (c + 1) * SWA_PAIR)
        for j in range(SWA_KV_HEADS):
            hs = slice(j * HEAD_DIM, (j + 1) * HEAD_DIM)
            qt = jnp.concatenate(
                [qt_ref[(j * SWA_GROUP + g) * HEAD_DIM:(j * SWA_GROUP + g + 1) * HEAD_DIM, rs]
                 for g in range(SWA_GROUP)], axis=1)
            s = _dot(kb[:, hs], qt) + bias
            sink = sink_ref[j:j + 1, :] * LOG2E
            m = jnp.maximum(jnp.max(s, axis=0, keepdims=True), sink)
            p = jnp.exp2(s - m)
            den = jnp.sum(p, axis=0, keepdims=True) + jnp.exp2(sink - m)
            o = _dot(vtb[hs, :], p.astype(bf16)) * (1.0 / den)
            for gp in range(SWA_GROUP // 2):
                two_heads = jnp.concatenate([o[:, (2 * gp) * SWA_PAIR:(2 * gp + 1) * SWA_PAIR],
                                             o[:, (2 * gp + 1) * SWA_PAIR:(2 * gp + 2) * SWA_PAIR]], axis=0)
                col = (j * SWA_GROUP + 2 * gp) * HEAD_DIM
                o_ref[rs, col:col + 2 * HEAD_DIM] = two_heads.T.astype(bf16)


def _swa(qt, k, vt, sink_rows, B, S, npair):
    T = k.shape[0]
    tq = npair * SWA_PAIR
    nt = S // tq
    return pl.pallas_call(
        functools.partial(_swa_kernel, npair=npair),
        grid=(B, nt),
        in_specs=[
            pl.BlockSpec((SWA_Q_COLS, tq), lambda b, i: (b, i)),
            pl.BlockSpec((S, SWA_KV_COLS), lambda b, i: (b, 0)),
            pl.BlockSpec((SWA_KV_COLS, S), lambda b, i: (b, 0)),
            pl.BlockSpec(sink_rows.shape, lambda b, i: (0, 0)),
        ],
        out_specs=pl.BlockSpec((tq, SWA_Q_COLS), lambda b, i: (b * nt + i, 0)),
        out_shape=jax.ShapeDtypeStruct((T, SWA_Q_COLS), bf16),
        compiler_params=_params(("parallel", "arbitrary")),
        name="swa",
    )(qt, k, vt, sink_rows)


def _mem_kv_kernel(m_ref, w_ref, o_ref):
    o_ref[...] = _dot(m_ref[...].astype(bf16), w_ref[...].astype(bf16)).astype(bf16)


def _mem_kv_all_layers(memf, w_kv):
    M = memf.shape[0]
    layers = w_kv.shape[0]
    return pl.pallas_call(
        _mem_kv_kernel,
        grid=(layers,),
        in_specs=[pl.BlockSpec((M, D_MODEL), lambda l: (0, 0)),
                  pl.BlockSpec((None, D_MODEL, 2 * D_MODEL), lambda l: (l, 0, 0))],
        out_specs=pl.BlockSpec((None, M, 2 * D_MODEL), lambda l: (l, 0, 0)),
        out_shape=jax.ShapeDtypeStruct((layers, M, 2 * D_MODEL), bf16),
        compiler_params=_params(("parallel",)),
        name="mem_kv",
    )(memf, w_kv)


FFN_COL_TILE = 256
HALO_ROWS = 16
GELU_C = float(np.sqrt(2.0 / np.pi))


def _mem_attention_block(x, wq_ref, kv_ref, wo_ref):
    q = _dot(x.astype(bf16), wq_ref[...]).astype(bf16)
    heads = []
    for h in range(MEM_HEADS):
        hs = slice(h * MEM_DH, (h + 1) * MEM_DH)
        s = lax.dot_general(q[:, hs], kv_ref[:, hs], _NT, preferred_element_type=f32)
        p = jnp.exp(s - jnp.max(s, axis=1, keepdims=True))
        inv = 1.0 / jnp.sum(p, axis=1, keepdims=True)
        o = _dot(p.astype(bf16), kv_ref[:, D_MODEL + h * MEM_DH:D_MODEL + (h + 1) * MEM_DH]) * inv
        heads.append(o.astype(bf16))
    return _dot(jnp.concatenate(heads, axis=1), wo_ref[...])


def _conv_ffn_block(x, x_prev, wi_ref, cw_ref, cb_ref, wo_ref, h_ref):
    tm = x.shape[0]
    xb = x.astype(bf16)
    x_all = jnp.concatenate([x_prev.astype(bf16), xb], axis=0)
    for f in range(D_FF // FFN_COL_TILE):
        cs = slice(f * FFN_COL_TILE, (f + 1) * FFN_COL_TILE)
        g_all = _dot(x_all, wi_ref[:, cs])
        u = _dot(xb, wi_ref[:, D_FF + f * FFN_COL_TILE:D_FF + (f + 1) * FFN_COL_TILE])
        y = (cw_ref[0:1, cs] * g_all[HALO_ROWS - 2:HALO_ROWS - 2 + tm, :]
             + cw_ref[1:2, cs] * g_all[HALO_ROWS - 1:HALO_ROWS - 1 + tm, :]
             + cw_ref[2:3, cs] * g_all[HALO_ROWS:, :] + cb_ref[:, cs])
        inner = y * (GELU_C + (0.044715 * GELU_C) * (y * y))
        h_ref[:, cs] = ((0.5 * y) * (1.0 + jnp.tanh(inner)) * u).astype(bf16)
    return _dot(h_ref[...], wo_ref[...])


def _post_kernel(*refs, n_act, tiles_per_seq):
    acts = refs[:n_act]
    (w_mix_ref, x_ref, lng_ref, lnb_ref, wq_ref, kv_ref, wo_ref, wi_ref, cw_ref, cb_ref, wf_ref,
     o_ref, h_ref, halo_ref) = refs[n_act:]
    tm = x_ref.shape[0]
    i = pl.program_id(0)

    @pl.when(i == 0)
    def _():
        halo_ref[...] = jnp.zeros_like(halo_ref)

    h = None
    off = 0
    for a_ref in acts:
        k = a_ref.shape[1]
        part = _dot(a_ref[...], w_mix_ref[off:off + k, :])
        h = part if h is None else h + part
        off += k
    x = _deepnorm_ln(x_ref[...], h, lng_ref[0:1, :], lnb_ref[0:1, :])
    x = _deepnorm_ln(x, _mem_attention_block(x, wq_ref, kv_ref, wo_ref), lng_ref[1:2, :], lnb_ref[1:2, :])
    x_prev = jnp.where((i % tiles_per_seq) != 0, halo_ref[...], 0.0)
    halo_ref[...] = x[tm - HALO_ROWS:, :]
    h = _conv_ffn_block(x, x_prev, wi_ref, cw_ref, cb_ref, wf_ref, h_ref)
    o_ref[...] = _deepnorm_ln(x, h, lng_ref[2:3, :], lnb_ref[2:3, :])


def _post_mixer(acts, w_mix, x, ln_g, ln_b, wq, kv, wo, wi, cw, cb, wf, B, S, tm):
    T = x.shape[0]
    tiles_per_seq = S // tm
    M = kv.shape[0] // B
    resident = lambda a: pl.BlockSpec(a.shape, lambda i: (0, 0), pipeline_mode=pl.Buffered(1))
    small = lambda a: pl.BlockSpec(a.shape, lambda i: (0, 0))
    return pl.pallas_call(
        functools.partial(_post_kernel, n_act=len(acts), tiles_per_seq=tiles_per_seq),
        grid=(T // tm,),
        in_specs=[pl.BlockSpec((tm, a.shape[1]), lambda i: (i, 0)) for a in acts] + [
            resident(w_mix),
            pl.BlockSpec((tm, D_MODEL), lambda i: (i, 0)),
            small(ln_g), small(ln_b),
            resident(wq),
            pl.BlockSpec((M, 2 * D_MODEL), lambda i: (i // tiles_per_seq, 0)),
            resident(wo), resident(wi), small(cw), small(cb), resident(wf),
        ],
        out_specs=pl.BlockSpec((tm, D_MODEL), lambda i: (i, 0)),
        out_shape=jax.ShapeDtypeStruct((T, D_MODEL), f32),
        scratch_shapes=[pltpu.VMEM((tm, D_FF), bf16), pltpu.VMEM((HALO_ROWS, D_MODEL), f32)],
        compiler_params=_params(("arbitrary",)),
        name="post_mixer",
    )(*acts, w_mix, x, ln_g, ln_b, wq, kv, wo, wi, cw, cb, wf)


def _even_weight(w_in):
    qa, ka, va, ra, ga, qb, kb, vb = jnp.split(
        w_in, np.cumsum([256, 256, 512, 512, GLA_GATE_RANK, 512, 512, 512])[:-1].tolist(), axis=1)
    pad = jnp.zeros((D_MODEL, LANES - GLA_GATE_RANK), w_in.dtype)
    return jnp.concatenate(
        [qa * GLA_DK ** -0.5, ka, ra, va, qb * HEAD_DIM ** -0.5, kb, ga, pad, vb], axis=1).astype(bf16)


def _odd_weights(w_in):
    wq, wk, wv = w_in[:, :SWA_Q_COLS], w_in[:, SWA_Q_COLS:SWA_Q_COLS + SWA_KV_COLS], w_in[:, SWA_Q_COLS + SWA_KV_COLS:]
    return (wq * HEAD_DIM ** -0.5).T.astype(bf16), wk.astype(bf16), wv.T.astype(bf16)


def kernel(x, mem, positions, even_w_in, even_w_out, gla_gate_w, gla_gate_b, gla_norm_g, diff_lam_q1, diff_lam_k1, diff_lam_q2, diff_lam_k2, diff_norm_g, odd_w_in, odd_w_out, swa_sinks, mem_w_q, mem_w_kv, mem_w_out, ffn_w_in, ffn_conv_w, ffn_conv_b, ffn_w_out, ln_g, ln_b):
    B, S, D = x.shape
    T = B * S
    tm = min(512, S)
    xf = x.reshape(T, D)

    inv_freq = ROPE_THETA ** (-jnp.arange(0, HEAD_DIM, 2, dtype=f32) / HEAD_DIM)
    ang = positions.astype(f32).reshape(T, 1) * inv_freq
    cos, sin = jnp.cos(ang), jnp.sin(ang)
    cos_t = jnp.tile(cos, (1, 4))
    sin_t = jnp.concatenate([-sin, sin, -sin, sin], axis=1)

    memf = mem.reshape(B * mem.shape[1], D)
    kv_all = _mem_kv_all_layers(memf, mem_w_kv)
    row2 = lambda v: v.reshape(1, -1)

    for i in range(DEPTH):
        j = i // 2
        if i % 2 == 0:
            lam_init = 0.8 - 0.6 * float(np.exp(-0.3 * i))
            of, ob, vt = _even_in_proj(xf, _even_weight(even_w_in[j]), cos_t, sin_t, B, S, tm)
            gate_w = jnp.pad(gla_gate_w[j], ((0, LANES - GLA_GATE_RANK), (0, 0))).astype(bf16)
            o_a = _gla(of, ob, gate_w, row2(gla_gate_b[j]), row2(gla_norm_g[j]), B, S, min(4, S // CHUNK))
            o_b = _diff_attention(ob, vt, row2(diff_lam_q1[j]), row2(diff_lam_k1[j]), row2(diff_lam_q2[j]),
                                  row2(diff_lam_k2[j]), diff_norm_g[j].reshape(-1, 1), lam_init, B, S,
                                  min(1024, S))
            acts, w_mix = [o_a, o_b], even_w_out[j]
        else:
            qt, k, vt = _odd_in_proj(xf, *_odd_weights(odd_w_in[j]), cos_t, sin_t, cos.T, sin.T, B, S, tm)
            sink_rows = jnp.repeat(swa_sinks[j].reshape(SWA_KV_HEADS, SWA_GROUP), SWA_PAIR, axis=1)
            acts, w_mix = [_swa(qt, k, vt, sink_rows, B, S, min(4, S // SWA_PAIR))], odd_w_out[j]

        kv = kv_all[i]
        xf = _post_mixer(acts, w_mix.astype(bf16), xf, ln_g[i], ln_b[i],
                         (mem_w_q[i] * MEM_DH ** -0.5).astype(bf16), kv, mem_w_out[i].astype(bf16),
                         ffn_w_in[i].astype(bf16), ffn_conv_w[i], row2(ffn_conv_b[i]), ffn_w_out[i].astype(bf16),
                         B, S, tm)
    return xf.reshape(B, S, D)
```
